```python
import jax
import jax.numpy as jnp
from jax import lax
import numpy as np

D_MODEL = 2048
BATCH = 8
SEQ = 2048
DEPTH = 2

N_A_LAYERS = DEPTH // 2
N_B_LAYERS = DEPTH - N_A_LAYERS
NORM_EPS = 1e-6

GDN_HEAD_DIM = 128
GDN_K_HEADS = D_MODEL // 128
GDN_V_HEADS = 2 * GDN_K_HEADS
GDN_KDIM = GDN_K_HEADS * GDN_HEAD_DIM
GDN_VDIM = GDN_V_HEADS * GDN_HEAD_DIM
GDN_CONV_CH = 2 * GDN_KDIM + GDN_VDIM
GDN_IN_DIM = GDN_CONV_CH + GDN_VDIM + 2 * GDN_V_HEADS
GDN_CONV = 4
GDN_CHUNK = 64

SB_HEAD_DIM = 128
SB_HEADS = D_MODEL // 128
SB_DIM = SB_HEADS * SB_HEAD_DIM
SB_BLOCK = 128

MOE_GROUPS = 4
MOE_EXPERTS_PER_GROUP = 8
MOE_EXPERTS = MOE_GROUPS * MOE_EXPERTS_PER_GROUP
MOE_TOP_K = 2
MOE_HIDDEN = 3 * D_MODEL // 8
MOE_BLOCK = 128

kernel_name = 'yoco_gdn_stickbreak_hmoe'


def rms_norm(x, gain):
    xf = x.astype(jnp.float32)
    y = xf * lax.rsqrt(jnp.mean(xf * xf, axis=-1, keepdims=True) + NORM_EPS)
    return (y * gain.astype(jnp.float32)).astype(x.dtype)


def l2_norm(x):
    return x * lax.rsqrt(jnp.sum(x * x, axis=-1, keepdims=True) + NORM_EPS)


def causal_depthwise_conv(x, w):
    taps, seq = w.shape[0], x.shape[1]
    xp = jnp.pad(x, ((0, 0), (taps - 1, 0), (0, 0)))
    y = xp[:, 0:seq] * w[0]
    for j in range(1, taps):
        y = y + xp[:, j:j + seq] * w[j]
    return y


def chunk_gated_delta_rule(q, k, v, g, beta):
    b, h, s, dk = q.shape
    dv = v.shape[-1]
    n, c = s // GDN_CHUNK, GDN_CHUNK
    q = (q * dk ** -0.5).reshape(b, h, n, c, dk)
    k = k.reshape(b, h, n, c, dk)
    v = v.reshape(b, h, n, c, dv)
    beta = beta.reshape(b, h, n, c)
    g = jnp.cumsum(g.reshape(b, h, n, c), axis=-1)
    incl = jnp.tril(jnp.ones((c, c), dtype=bool))
    strict = jnp.tril(jnp.ones((c, c), dtype=bool), -1)
    diff = g[..., :, None] - g[..., None, :]
    decay = jnp.where(incl, jnp.exp(jnp.where(incl, diff, 0.0)), 0.0)
    k_beta = k * beta[..., None]
    lower = jnp.where(strict, jnp.einsum('bhnid,bhnjd->bhnij', k_beta, k) * decay, 0.0)
    eye = jnp.broadcast_to(jnp.eye(c, dtype=q.dtype), lower.shape)
    t_inv = lax.linalg.triangular_solve(eye + lower, eye, left_side=True, lower=True, unit_diagonal=True)
    u = jnp.einsum('bhnij,bhnjv->bhniv', t_inv, v * beta[..., None])
    w = jnp.einsum('bhnij,bhnjd->bhnid', t_inv, k_beta * jnp.exp(g)[..., None])
    intra = jnp.where(incl, jnp.einsum('bhnid,bhnjd->bhnij', q, k) * decay, 0.0)

    def step(state, inp):
        q_c, k_c, u_c, w_c, a_c, g_c = inp
        v_new = u_c - jnp.einsum('bhcd,bhdv->bhcv', w_c, state)
        o_c = (jnp.einsum('bhcd,bhdv->bhcv', q_c * jnp.exp(g_c)[..., None], state)
               + jnp.einsum('bhcj,bhjv->bhcv', a_c, v_new))
        g_last = g_c[..., -1:]
        k_dec = k_c * jnp.exp(g_last - g_c)[..., None]
        state = state * jnp.exp(g_last)[..., None] + jnp.einsum('bhcd,bhcv->bhdv', k_dec, v_new)
        return state, o_c

    chunk_first = lambda t: jnp.moveaxis(t, 2, 0)
    state0 = jnp.zeros((b, h, dk, dv), q.dtype)
    _, o = lax.scan(step, state0, (chunk_first(q), chunk_first(k), chunk_first(u),
                                   chunk_first(w), chunk_first(intra), chunk_first(g)))
    return jnp.moveaxis(o, 0, 2).reshape(b, h, s, dv)


def gated_deltanet(h, w_in, conv_w, a_log, dt_bias, out_gain, w_out):
    b, s, _ = h.shape
    proj = h @ w_in
    qkv, z, beta_logit, a = jnp.split(
        proj, [GDN_CONV_CH, GDN_CONV_CH + GDN_VDIM, GDN_CONV_CH + GDN_VDIM + GDN_V_HEADS], axis=-1)
    qkv = jax.nn.silu(causal_depthwise_conv(qkv, conv_w)).astype(jnp.float32)
    q, k, v = jnp.split(qkv, [GDN_KDIM, 2 * GDN_KDIM], axis=-1)
    rep = GDN_V_HEADS // GDN_K_HEADS
    q = jnp.repeat(l2_norm(q.reshape(b, s, GDN_K_HEADS, GDN_HEAD_DIM)), rep, axis=2)
    k = jnp.repeat(l2_norm(k.reshape(b, s, GDN_K_HEADS, GDN_HEAD_DIM)), rep, axis=2)
    v = v.reshape(b, s, GDN_V_HEADS, GDN_HEAD_DIM)
    beta = jax.nn.sigmoid(beta_logit.astype(jnp.float32))
    g = -jnp.exp(a_log.astype(jnp.float32)) * jax.nn.softplus(
        a.astype(jnp.float32) + dt_bias.astype(jnp.float32))
    heads_first = lambda t: jnp.swapaxes(t, 1, 2)
    o = chunk_gated_delta_rule(heads_first(q), heads_first(k), heads_first(v),
                               heads_first(g), heads_first(beta))
    o = jnp.swapaxes(o, 1, 2)
    gate = jax.nn.silu(z.astype(jnp.float32)).reshape(b, s, GDN_V_HEADS, GDN_HEAD_DIM)
    o = rms_norm(o, out_gain) * gate
    return o.reshape(b, s, GDN_VDIM).astype(h.dtype) @ w_out


def shared_kv(h, norm_gain, w_kv, k_gain):
    b, s, _ = h.shape
    kv = rms_norm(h, norm_gain) @ w_kv
    k, v = jnp.split(kv, 2, axis=-1)
    k = rms_norm(k.reshape(b, s, SB_HEADS, SB_HEAD_DIM), k_gain)
    v = v.reshape(b, s, SB_HEADS, SB_HEAD_DIM)
    return jnp.swapaxes(k, 1, 2).astype(jnp.float32), jnp.swapaxes(v, 1, 2).astype(jnp.float32)


def stick_breaking_attention(q, k, v):
    s = q.shape[2]
    scale = q.shape[-1] ** -0.5
    outs = []
    for blk in range(s // SB_BLOCK):
        q0 = blk * SB_BLOCK
        q1 = q0 + SB_BLOCK
        kb, vb = k[:, :, :q1], v[:, :, :q1]
        z = jnp.einsum('bhqd,bhkd->bhqk', q[:, :, q0:q1], kb) * scale
        causal = jnp.arange(q1)[None, :] < (q0 + jnp.arange(SB_BLOCK))[:, None]
        log_fail = jnp.where(causal, -jax.nn.softplus(z), 0.0)
        later = lax.cumsum(log_fail, axis=3, reverse=True) - log_fail
        weight = jnp.where(causal, jnp.exp(jax.nn.log_sigmoid(z) + later), 0.0)
        outs.append(jnp.einsum('bhqk,bhkd->bhqd', weight, vb))
    return jnp.concatenate(outs, axis=2)


def stick_breaking_layer(h, w_q, q_gain, k, v, w_out):
    b, s, _ = h.shape
    q = rms_norm((h @ w_q).reshape(b, s, SB_HEADS, SB_HEAD_DIM), q_gain)
    q = jnp.swapaxes(q, 1, 2).astype(jnp.float32)
    o = stick_breaking_attention(q, k, v)
    o = jnp.swapaxes(o, 1, 2).reshape(b, s, SB_DIM).astype(h.dtype)
    return o @ w_out


def hierarchical_moe(h, w_rg, b_rg, w_re, b_re, w_gate, w_up, w_down):
    b, s, d = h.shape
    t = b * s
    xt = h.reshape(t, d)
    p_group = jax.nn.softmax((xt @ w_rg).astype(jnp.float32) + b_rg.astype(jnp.float32), axis=-1)
    g_val, g_idx = lax.top_k(p_group, 1)
    logits_e = ((xt @ w_re).astype(jnp.float32) + b_re.astype(jnp.float32)).reshape(
        t, MOE_GROUPS, MOE_EXPERTS_PER_GROUP)
    logits_sel = logits_e[jnp.arange(t), g_idx[:, 0]]
    e_val, e_idx = lax.top_k(jax.nn.softmax(logits_sel, axis=-1), MOE_TOP_K)
    gate = g_val * e_val / jnp.sum(e_val, axis=-1, keepdims=True)
    eid = (g_idx * MOE_EXPERTS_PER_GROUP + e_idx).reshape(-1)
    tok = jnp.repeat(jnp.arange(t, dtype=jnp.int32), MOE_TOP_K)
    wts = gate.reshape(-1)
    n_assign = t * MOE_TOP_K
    order = jnp.argsort(eid)
    eid_s, tok_s, wts_s = eid[order], tok[order], wts[order]
    counts = jnp.bincount(eid, length=MOE_EXPERTS)
    padded = (counts + MOE_BLOCK - 1) // MOE_BLOCK * MOE_BLOCK
    pad_end = jnp.cumsum(padded)
    pad_start = pad_end - padded
    start = jnp.cumsum(counts) - counts
    dest = pad_start[eid_s] + jnp.arange(n_assign, dtype=jnp.int32) - start[eid_s]
    n_blocks = -(-n_assign // MOE_BLOCK) + MOE_EXPERTS
    n_slots = n_blocks * MOE_BLOCK
    slot_tok = jnp.full((n_slots,), t, dtype=jnp.int32).at[dest].set(tok_s)
    x_ext = jnp.concatenate([xt, jnp.zeros((1, d), xt.dtype)], axis=0)
    xs = x_ext[slot_tok].reshape(n_blocks, MOE_BLOCK, d)
    blk_expert = jnp.minimum(
        jnp.searchsorted(pad_end, jnp.arange(n_blocks) * MOE_BLOCK, side='right'), MOE_EXPERTS - 1)

    def run_block(args):
        xb, e = args
        hb = jax.nn.silu(xb @ w_gate[e]) * (xb @ w_up[e])
        return hb @ w_down[e]

    ys = lax.map(run_block, (xs, blk_expert)).reshape(n_slots, d)
    contrib = ys[dest] * wts_s[:, None].astype(ys.dtype)
    out = jnp.zeros((t, d), ys.dtype).at[tok_s].add(contrib)
    return out.reshape(b, s, d)


def setup_inputs(seed: int = 0) -> dict:
    key = jax.random.key(seed)
    ks = jax.random.split(key, 23)
    d = D_MODEL
    nrm = lambda k, shape, scale: jax.random.normal(k, shape, jnp.float32) * scale
    gain = lambda k, shape: 1.0 + 0.02 * jax.random.normal(k, shape, jnp.float32)
    return {
        'x': nrm(ks[0], (BATCH, SEQ, d), 1.0),
        'norm_a': gain(ks[1], (N_A_LAYERS, d)),
        'w_in_a': nrm(ks[2], (N_A_LAYERS, d, GDN_IN_DIM), d ** -0.5),
        'conv_a': nrm(ks[3], (N_A_LAYERS, GDN_CONV, GDN_CONV_CH), GDN_CONV ** -0.5),
        'a_log_a': jnp.log(jax.random.uniform(ks[4], (N_A_LAYERS, GDN_V_HEADS), jnp.float32, 1.0, 16.0)),
        'dt_bias_a': nrm(ks[5], (N_A_LAYERS, GDN_V_HEADS), 0.1),
        'out_norm_a': gain(ks[6], (N_A_LAYERS, GDN_HEAD_DIM)),
        'w_out_a': nrm(ks[7], (N_A_LAYERS, GDN_VDIM, d), GDN_VDIM ** -0.5),
        'norm_kv': gain(ks[8], (d,)),
        'w_kv': nrm(ks[9], (d, 2 * SB_DIM), d ** -0.5),
        'k_norm_b': gain(ks[10], (SB_HEAD_DIM,)),
        'norm_b': gain(ks[11], (N_B_LAYERS, d)),
        'w_q_b': nrm(ks[12], (N_B_LAYERS, d, SB_DIM), d ** -0.5),
        'q_norm_b': gain(ks[13], (N_B_LAYERS, SB_HEAD_DIM)),
        'w_out_b': nrm(ks[14], (N_B_LAYERS, SB_DIM, d), SB_DIM ** -0.5),
        'norm_moe': gain(ks[15], (DEPTH, d)),
        'w_router_group': nrm(ks[16], (DEPTH, d, MOE_GROUPS), d ** -0.5),
        'b_router_group': nrm(ks[17], (DEPTH, MOE_GROUPS), 0.01),
        'w_router_expert': nrm(ks[18], (DEPTH, d, MOE_EXPERTS), d ** -0.5),
        'b_router_expert': nrm(ks[19], (DEPTH, MOE_EXPERTS), 0.01),
        'w_gate': nrm(ks[20], (DEPTH, MOE_EXPERTS, d, MOE_HIDDEN), d ** -0.5),
        'w_up': nrm(ks[21], (DEPTH, MOE_EXPERTS, d, MOE_HIDDEN), d ** -0.5),
        'w_down': nrm(ks[22], (DEPTH, MOE_EXPERTS, MOE_HIDDEN, d), MOE_HIDDEN ** -0.5),
    }


def reference(x, norm_a, w_in_a, conv_a, a_log_a, dt_bias_a, out_norm_a, w_out_a,
              norm_kv, w_kv, k_norm_b, norm_b, w_q_b, q_norm_b, w_out_b,
              norm_moe, w_router_group, b_router_group, w_router_expert, b_router_expert,
              w_gate, w_up, w_down):
    h = x
    k_shared = None
    v_shared = None
    for layer in range(DEPTH):
        if layer < N_A_LAYERS:
            i = layer
            h = h + gated_deltanet(rms_norm(h, norm_a[i]), w_in_a[i], conv_a[i], a_log_a[i],
                                   dt_bias_a[i], out_norm_a[i], w_out_a[i])
        else:
            i = layer - N_A_LAYERS
            if i == 0:
                k_shared, v_shared = shared_kv(h, norm_kv, w_kv, k_norm_b)
            h = h + stick_breaking_layer(rms_norm(h, norm_b[i]), w_q_b[i], q_norm_b[i],
                                         k_shared, v_shared, w_out_b[i])
        h = h + hierarchical_moe(rms_norm(h, norm_moe[layer]), w_router_group[layer],
                                 b_router_group[layer], w_router_expert[layer],
                                 b_router_expert[layer], w_gate[layer], w_up[layer], w_down[layer])
    return h
```

```python
import functools

import jax
import jax.numpy as jnp
from jax import lax
from jax.experimental import pallas as pl
from jax.experimental.pallas import tpu as pltpu

NORM_EPS = 1e-6
HEAD_DIM = 128
GDN_CONV = 4
GDN_CHUNK = 64
GDN_GROUP = 4
SB_TILE = 256
MOE_GROUPS = 4
MOE_EXPERTS_PER_GROUP = 8
MOE_EXPERTS = MOE_GROUPS * MOE_EXPERTS_PER_GROUP
MOE_TOP_K = 2
MOE_ROWS = 256
LANES = 128
VMEM_LIMIT = 56 * 1024 * 1024

BF16 = jnp.bfloat16
F32 = jnp.float32


def _params(*semantics):
    return pltpu.CompilerParams(dimension_semantics=semantics, vmem_limit_bytes=VMEM_LIMIT)


def _silu(x):
    return x * (1.0 / (1.0 + jnp.exp(-x)))


def _softplus(x):
    return jnp.maximum(x, 0.0) + jnp.log(1.0 + jnp.exp(-jnp.abs(x)))


def _rms_rows(x):
    return x * lax.rsqrt(jnp.mean(x * x, axis=-1, keepdims=True) + NORM_EPS)


def _norm_matmul_kernel(x_ref, g_ref, w_ref, o_ref, xn_ref):
    @pl.when(pl.program_id(1) == 0)
    def _():
        xn_ref[...] = (_rms_rows(x_ref[...]) * g_ref[...]).astype(BF16)

    o_ref[...] = jnp.dot(xn_ref[...], w_ref[...], preferred_element_type=F32).astype(o_ref.dtype)


def norm_matmul(x, gain, w, *, tm=512, tn=512, out_dtype=F32):
    t, d = x.shape
    n = w.shape[1]
    tn = min(tn, n)
    return pl.pallas_call(
        _norm_matmul_kernel,
        grid=(t // tm, n // tn),
        in_specs=[pl.BlockSpec((tm, d), lambda i, j: (i, 0)),
                  pl.BlockSpec((1, d), lambda i, j: (0, 0)),
                  pl.BlockSpec((d, tn), lambda i, j: (0, j))],
        out_specs=pl.BlockSpec((tm, tn), lambda i, j: (i, j)),
        out_shape=jax.ShapeDtypeStruct((t, n), out_dtype),
        scratch_shapes=[pltpu.VMEM((tm, d), BF16)],
        compiler_params=_params("arbitrary", "arbitrary"),
        name="norm_matmul",
    )(x, gain.reshape(1, d), w)


def _matmul_residual_kernel(a_ref, w_ref, r_ref, o_ref):
    o_ref[...] = r_ref[...] + jnp.dot(a_ref[...], w_ref[...], preferred_element_type=F32)


def matmul_residual(a, w, res, *, tm=512, tn=512):
    t, k = a.shape
    n = w.shape[1]
    tn = min(tn, n)
    return pl.pallas_call(
        _matmul_residual_kernel,
        grid=(t // tm, n // tn),
        in_specs=[pl.BlockSpec((tm, k), lambda i, j: (i, 0)),
                  pl.BlockSpec((k, tn), lambda i, j: (0, j)),
                  pl.BlockSpec((tm, tn), lambda i, j: (i, j))],
        out_specs=pl.BlockSpec((tm, tn), lambda i, j: (i, j)),
        out_shape=jax.ShapeDtypeStruct((t, n), F32),
        compiler_params=_params("arbitrary", "arbitrary"),
        name="matmul_residual",
    )(a, w, res)


def _gdn_gates_kernel(x_ref, g_ref, w_ref, alog_ref, dt_ref, o_ref, *, n_heads):
    xn = (_rms_rows(x_ref[...]) * g_ref[...]).astype(BF16)
    y = jnp.dot(xn, w_ref[...], preferred_element_type=F32)
    lane = lax.broadcasted_iota(jnp.int32, y.shape, 1)
    row = lax.broadcasted_iota(jnp.int32, y.shape, 0)
    beta = 1.0 / (1.0 + jnp.exp(-y))
    g = -jnp.exp(alog_ref[...]) * _softplus(y + dt_ref[...])
    pos = row & (GDN_CHUNK - 1)
    shift = 1
    while shift < GDN_CHUNK:
        g = g + jnp.where(pos >= shift, pltpu.roll(g, shift, axis=0), 0.0)
        shift *= 2
    o_ref[...] = jnp.where(lane < n_heads, beta, jnp.where(lane < 2 * n_heads, g, 0.0))


def gdn_gates(x, gain, w_small, a_log, dt_bias, *, tm=512):
    t, d = x.shape
    hv = a_log.shape[0]
    w_pad = jnp.zeros((d, LANES), BF16).at[:, :2 * hv].set(w_small.astype(BF16))
    row_param = lambda p: jnp.zeros((1, LANES), F32).at[0, hv:2 * hv].set(p.astype(F32))
    return pl.pallas_call(
        functools.partial(_gdn_gates_kernel, n_heads=hv),
        grid=(t // tm,),
        in_specs=[pl.BlockSpec((tm, d), lambda i: (i, 0)),
                  pl.BlockSpec((1, d), lambda i: (0, 0)),
                  pl.BlockSpec((d, LANES), lambda i: (0, 0)),
                  pl.BlockSpec((1, LANES), lambda i: (0, 0)),
                  pl.BlockSpec((1, LANES), lambda i: (0, 0))],
        out_specs=pl.BlockSpec((tm, LANES), lambda i: (i, 0)),
        out_shape=jax.ShapeDtypeStruct((t, LANES), F32),
        compiler_params=_params("arbitrary"),
        name="gdn_gates",
    )(x, gain.reshape(1, d), w_pad, row_param(a_log), row_param(dt_bias))


def _unit_lower_inverse(low):
    c = low.shape[0]
    ri = lax.broadcasted_iota(jnp.int32, (c, c), 0)
    ci = lax.broadcasted_iota(jnp.int32, (c, c), 1)
    eye = jnp.where(ri == ci, 1.0, 0.0).astype(F32)
    x = eye
    s = 1
    while s < c:
        join = ((ri & -(2 * s)) == (ci & -(2 * s))) & ((ri & s) != 0) & ((ci & s) == 0)
        cs = jnp.where(join, low, 0.0)
        if s == 1:
            x = eye - cs
        else:
            x = x - jnp.dot(x, jnp.dot(cs, x, preferred_element_type=F32), preferred_element_type=F32)
        s *= 2
    return x


def _gdn_kernel(q_ref, k_ref, v_ref, z_ref, gb_ref, gct_ref, cq_ref, ck_ref, cv_ref, gain_ref, o_ref,
                pad_ref, qs_ref, ks_ref, kt_ref, vs_ref, bcol_ref, gcol_ref,
                u_ref, w_ref, a_ref, st_ref, *, n_vheads):
    seq = q_ref.shape[0]
    n_chunks = seq // GDN_CHUNK
    c = GDN_CHUNK
    h = pl.program_id(1)

    def conv_silu(x_ref, cw_ref, width):
        pad_ref[0:8, 0:width] = jnp.zeros((8, width), F32)
        pad_ref[8:8 + seq, 0:width] = x_ref[...]
        y = pad_ref[5:5 + seq, 0:width] * cw_ref[0:1, :]
        for j in range(1, GDN_CONV):
            y = y + pad_ref[5 + j:5 + j + seq, 0:width] * cw_ref[j:j + 1, :]
        return _silu(y)

    def l2n(x):
        return x * lax.rsqrt(jnp.sum(x * x, axis=-1, keepdims=True) + NORM_EPS)

    qs_ref[...] = l2n(conv_silu(q_ref, cq_ref, HEAD_DIM)) * (HEAD_DIM ** -0.5)
    ks_ref[...] = l2n(conv_silu(k_ref, ck_ref, HEAD_DIM))
    vs_ref[...] = conv_silu(v_ref, cv_ref, 2 * HEAD_DIM)
    for ch in range(n_chunks):
        kt_ref[ch] = ks_ref[ch * c:(ch + 1) * c, :].T

    gb = gb_ref[...]
    lane = lax.broadcasted_iota(jnp.int32, gb.shape, 1)
    for r in range(2):
        hv = 2 * h + r
        bcol = jnp.sum(jnp.where(lane == hv, gb, 0.0), axis=1, keepdims=True)
        gcol = jnp.sum(jnp.where(lane == n_vheads + hv, gb, 0.0), axis=1, keepdims=True)
        bcol_ref[r] = jnp.broadcast_to(bcol, (seq, HEAD_DIM))
        gcol_ref[r] = jnp.broadcast_to(gcol, (seq, HEAD_DIM))

    ri = lax.broadcasted_iota(jnp.int32, (c, c), 0)
    ci = lax.broadcasted_iota(jnp.int32, (c, c), 1)
    incl = ri >= ci
    strict = ri > ci

    def phase1(grp, carry):
        for gi in range(GDN_GROUP):
            ch = grp * GDN_GROUP + gi
            rows = pl.ds(pl.multiple_of(ch * c, c), c)
            qc = qs_ref[rows, :]
            kc = ks_ref[rows, :]
            kt = kt_ref[ch]
            qkk = jnp.dot(jnp.concatenate([qc, kc], axis=0), kt, preferred_element_type=F32)
            qk, kk = qkk[:c], qkk[c:]
            for r in range(2):
                b = bcol_ref[r, rows, :]
                gcf = gcol_ref[r, rows, :]
                grow = gct_ref[0, r, pl.ds(ch, 1), :]
                diff = gcf[:, :c] - grow
                decay = jnp.where(incl, jnp.exp(jnp.where(incl, diff, 0.0)), 0.0)
                low = jnp.where(strict, b[:, :c] * kk * decay, 0.0)
                tinv = _unit_lower_inverse(low)
                vc = vs_ref[rows, r * HEAD_DIM:(r + 1) * HEAD_DIM]
                rhs = jnp.concatenate([vc * b, kc * b * jnp.exp(gcf)], axis=1)
                uw = jnp.dot(tinv, rhs, preferred_element_type=F32)
                u_ref[r, rows, :] = uw[:, :HEAD_DIM]
                w_ref[r, rows, :] = uw[:, HEAD_DIM:]
                a_ref[r, rows, :] = qk * decay
        return carry

    lax.fori_loop(0, n_chunks // GDN_GROUP, phase1, 0)

    st_ref[...] = jnp.zeros(st_ref.shape, F32)
    gain = gain_ref[...]

    def phase2(ch, carry):
        rows = pl.ds(pl.multiple_of(ch * c, c), c)
        qc = qs_ref[rows, :]
        kt = kt_ref[ch]
        for r in range(2):
            gcf = gcol_ref[r, rows, :]
            grow = gct_ref[0, r, pl.ds(ch, 1), :]
            glast = gcf[c - 1:c, :]
            state = st_ref[r]
            wq = jnp.concatenate([w_ref[r, rows, :], qc * jnp.exp(gcf)], axis=0)
            ws = jnp.dot(wq, state, preferred_element_type=F32)
            v_new = u_ref[r, rows, :] - ws[:c]
            o = ws[c:] + jnp.dot(a_ref[r, rows, :], v_new, preferred_element_type=F32)
            kdec_t = kt * jnp.exp(glast[:, :c] - grow)
            st_ref[r] = state * jnp.exp(glast) + jnp.dot(kdec_t, v_new, preferred_element_type=F32)
            zc = z_ref[rows, r * HEAD_DIM:(r + 1) * HEAD_DIM]
            o_ref[rows, r * HEAD_DIM:(r + 1) * HEAD_DIM] = (_rms_rows(o) * gain * _silu(zc)).astype(o_ref.dtype)
        return carry

    lax.fori_loop(0, n_chunks, phase2, 0)


def gdn_core(proj, gates, conv_w, out_gain, *, batch, seq, n_kheads):
    t = proj.shape[0]
    n_vheads = 2 * n_kheads
    kdim = n_kheads * HEAD_DIM
    vdim = 2 * kdim
    n_chunks = seq // GDN_CHUNK
    c = GDN_CHUNK
    gct = gates[:, n_vheads:2 * n_vheads].reshape(batch, n_chunks, c, n_kheads, 2)
    gct = gct.transpose(0, 3, 4, 1, 2).reshape(batch * n_kheads, 2, n_chunks, c)
    kq = kdim // HEAD_DIM
    return pl.pallas_call(
        functools.partial(_gdn_kernel, n_vheads=n_vheads),
        grid=(batch, n_kheads),
        in_specs=[pl.BlockSpec((seq, HEAD_DIM), lambda b, h: (b, h)),
                  pl.BlockSpec((seq, HEAD_DIM), lambda b, h: (b, kq + h)),
                  pl.BlockSpec((seq, 2 * HEAD_DIM), lambda b, h: (b, kq + h)),
                  pl.BlockSpec((seq, 2 * HEAD_DIM), lambda b, h: (b, 2 * kq + h)),
                  pl.BlockSpec((seq, LANES), lambda b, h: (b, 0)),
                  pl.BlockSpec((1, 2, n_chunks, c), lambda b, h: (b * n_kheads + h, 0, 0, 0)),
                  pl.BlockSpec((GDN_CONV, HEAD_DIM), lambda b, h: (0, h)),
                  pl.BlockSpec((GDN_CONV, HEAD_DIM), lambda b, h: (0, kq + h)),
                  pl.BlockSpec((GDN_CONV, 2 * HEAD_DIM), lambda b, h: (0, kq + h)),
                  pl.BlockSpec((1, HEAD_DIM), lambda b, h: (0, 0))],
        out_specs=pl.BlockSpec((seq, 2 * HEAD_DIM), lambda b, h: (b, h)),
        out_shape=jax.ShapeDtypeStruct((t, vdim), BF16),
        scratch_shapes=[pltpu.VMEM((seq + 8, 2 * HEAD_DIM), F32),
                        pltpu.VMEM((seq, HEAD_DIM), F32),
                        pltpu.VMEM((seq, HEAD_DIM), F32),
                        pltpu.VMEM((n_chunks, HEAD_DIM, c), F32),
                        pltpu.VMEM((seq, 2 * HEAD_DIM), F32),
                        pltpu.VMEM((2, seq, HEAD_DIM), F32),
                        pltpu.VMEM((2, seq, HEAD_DIM), F32),
                        pltpu.VMEM((2, seq, HEAD_DIM), F32),
                        pltpu.VMEM((2, seq, HEAD_DIM), F32),
                        pltpu.VMEM((2, seq, c), F32),
                        pltpu.VMEM((2, HEAD_DIM, HEAD_DIM), F32)],
        compiler_params=_params("arbitrary", "arbitrary"),
        name="gdn_core",
    )(proj, proj, proj, proj, gates, gct, conv_w, conv_w, conv_w, out_gain.reshape(1, HEAD_DIM))


def _sb_kernel(q_ref, k_ref, v_ref, qg_ref, kg_ref, o_ref, qs_ref, ks_ref, vs_ref, tri_ref):
    seq = q_ref.shape[0]
    tq = SB_TILE
    qs_ref[...] = (_rms_rows(q_ref[...]) * qg_ref[...] * (HEAD_DIM ** -0.5)).astype(BF16)
    ks_ref[...] = (_rms_rows(k_ref[...]) * kg_ref[...]).astype(BF16)
    vs_ref[...] = v_ref[...].astype(BF16)
    ri = lax.broadcasted_iota(jnp.int32, (tq, tq), 0)
    ci = lax.broadcasted_iota(jnp.int32, (tq, tq), 1)
    tri = jnp.where(ri > ci, 1.0, 0.0).astype(BF16)
    tri_ref[0:tq, :] = tri
    tri_ref[tq:2 * tq, :] = tri
    causal = ci < ri

    def tile(qb, rows_k, later0, mask):
        z = lax.dot_general(qb, ks_ref[rows_k, :], (((1,), (1,)), ((), ())), preferred_element_type=F32)
        lf = -_softplus(z)
        if mask is not None:
            lf = jnp.where(mask, lf, 0.0)
        lf_hi = lf.astype(BF16)
        lf_lo = (lf - lf_hi.astype(F32)).astype(BF16)
        later = jnp.dot(jnp.concatenate([lf_hi, lf_lo], axis=1), tri_ref[...],
                        preferred_element_type=F32) + later0
        wgt = jnp.exp(z + lf + later)
        if mask is not None:
            wgt = jnp.where(mask, wgt, 0.0)
        pv = jnp.dot(wgt.astype(BF16), vs_ref[rows_k, :], preferred_element_type=F32)
        return pv, later0 + jnp.sum(lf, axis=1, keepdims=True)

    for qi in range(seq // tq):
        rows_q = pl.ds(qi * tq, tq)
        qb = qs_ref[rows_q, :]
        acc, later0 = tile(qb, rows_q, jnp.zeros((tq, 1), F32), causal)

        def body(step, carry, qb=qb, qi=qi):
            acc, later0 = carry
            kj = qi - 1 - step
            rows_k = pl.ds(pl.multiple_of(kj * tq, tq), tq)
            pv, later0 = tile(qb, rows_k, later0, None)
            return acc + pv, later0

        acc, _ = lax.fori_loop(0, qi, body, (acc, later0))
        o_ref[rows_q, :] = acc.astype(o_ref.dtype)


def sb_attention(q_raw, kv_raw, q_gain, k_gain, *, batch, seq, n_heads):
    t = q_raw.shape[0]
    return pl.pallas_call(
        _sb_kernel,
        grid=(batch, n_heads),
        in_specs=[pl.BlockSpec((seq, HEAD_DIM), lambda b, h: (b, h)),
                  pl.BlockSpec((seq, HEAD_DIM), lambda b, h: (b, h)),
                  pl.BlockSpec((seq, HEAD_DIM), lambda b, h: (b, n_heads + h)),
                  pl.BlockSpec((1, HEAD_DIM), lambda b, h: (0, 0)),
                  pl.BlockSpec((1, HEAD_DIM), lambda b, h: (0, 0))],
        out_specs=pl.BlockSpec((seq, HEAD_DIM), lambda b, h: (b, h)),
        out_shape=jax.ShapeDtypeStruct((t, n_heads * HEAD_DIM), BF16),
        scratch_shapes=[pltpu.VMEM((seq, HEAD_DIM), BF16),
                        pltpu.VMEM((seq, HEAD_DIM), BF16),
                        pltpu.VMEM((seq, HEAD_DIM), BF16),
                        pltpu.VMEM((2 * SB_TILE, SB_TILE), BF16)],
        compiler_params=_params("arbitrary", "arbitrary"),
        name="sb_attention",
    )(q_raw, kv_raw, kv_raw, q_gain.reshape(1, HEAD_DIM), k_gain.reshape(1, HEAD_DIM))


def _router_kernel(x_ref, g_ref, w_ref, b_ref, xn_ref, gate_ref, eid_ref):
    xn = _rms_rows(x_ref[...]) * g_ref[...]
    xn_ref[...] = xn
    x_hi = xn.astype(BF16)
    x_lo = (xn - x_hi.astype(F32)).astype(BF16)
    logits = jnp.dot(jnp.concatenate([x_hi, x_lo, x_hi], axis=1), w_ref[...],
                     preferred_element_type=F32) + b_ref[...]
    lane = lax.broadcasted_iota(jnp.int32, logits.shape, 1).astype(F32)
    neg = -jnp.inf
    first_lane = lambda hit: jnp.min(jnp.where(hit, lane, float(LANES)), axis=1, keepdims=True)
    lg = jnp.where(lane < MOE_GROUPS, logits, neg)
    mg = jnp.max(lg, axis=1, keepdims=True)
    g_val = 1.0 / jnp.sum(jnp.exp(lg - mg), axis=1, keepdims=True)
    g_idx = first_lane(lg == mg)
    lo = MOE_GROUPS + MOE_EXPERTS_PER_GROUP * g_idx
    le = jnp.where((lane >= lo) & (lane < lo + MOE_EXPERTS_PER_GROUP), logits, neg)
    m1 = jnp.max(le, axis=1, keepdims=True)
    i1 = first_lane(le == m1)
    le2 = jnp.where(lane == i1, neg, le)
    m2 = jnp.max(le2, axis=1, keepdims=True)
    i2 = first_lane(le2 == m2)
    den = jnp.sum(jnp.exp(le - m1), axis=1, keepdims=True)
    e1 = 1.0 / den
    e2 = jnp.exp(m2 - m1) / den
    w1 = g_val * e1 / (e1 + e2)
    w2 = g_val * e2 / (e1 + e2)
    gate_ref[...] = jnp.where(lane == 0, w1, jnp.where(lane == 1, w2, 0.0))
    eid = jnp.where(lane == 0, i1 - MOE_GROUPS, jnp.where(lane == 1, i2 - MOE_GROUPS, 0.0))
    eid_ref[...] = eid.astype(jnp.int32)


def moe_router(h, gain, w_rg, b_rg, w_re, b_re, *, tm=512):
    t, d = h.shape
    n_logits = MOE_GROUPS + MOE_EXPERTS
    w = jnp.zeros((d, LANES), F32).at[:, :MOE_GROUPS].set(w_rg).at[:, MOE_GROUPS:n_logits].set(w_re)
    w_hi = w.astype(BF16)
    w_lo = (w - w_hi.astype(F32)).astype(BF16)
    w3 = jnp.concatenate([w_hi, w_hi, w_lo], axis=0)
    bias = jnp.zeros((1, LANES), F32).at[0, :MOE_GROUPS].set(b_rg).at[0, MOE_GROUPS:n_logits].set(b_re)
    return pl.pallas_call(
        _router_kernel,
        grid=(t // tm,),
        in_specs=[pl.BlockSpec((tm, d), lambda i: (i, 0)),
                  pl.BlockSpec((1, d), lambda i: (0, 0)),
                  pl.BlockSpec((3 * d, LANES), lambda i: (0, 0)),
                  pl.BlockSpec((1, LANES), lambda i: (0, 0))],
        out_specs=[pl.BlockSpec((tm, d), lambda i: (i, 0)),
                   pl.BlockSpec((tm, LANES), lambda i: (i, 0)),
                   pl.BlockSpec((tm, LANES), lambda i: (i, 0))],
        out_shape=[jax.ShapeDtypeStruct((t, d), F32),
                   jax.ShapeDtypeStruct((t, LANES), F32),
                   jax.ShapeDtypeStruct((t, LANES), jnp.int32)],
        compiler_params=_params("arbitrary"),
        name="moe_router",
    )(h, gain.reshape(1, d), w3, bias)


def _gather_rows(src_hbm, idx_ref, base, dst_ref, sem, n_rows):
    def issue(r, carry):
        pltpu.make_async_copy(src_hbm.at[pl.ds(idx_ref[base + r], 1), :],
                              dst_ref.at[pl.ds(r, 1), :], sem).start()
        return carry
    lax.fori_loop(0, n_rows, issue, 0)


def _wait_rows(src_hbm, dst_ref, sem, n_rows):
    pltpu.make_async_copy(src_hbm.at[pl.ds(0, n_rows), :], dst_ref, sem).wait()


def _expert_kernel(blk_expert_ref, slot_tok_ref, n_used_ref, x_hbm, wg_ref, wu_ref, wd_ref, y_ref,
                   xbuf_ref, sem_ref):
    i = pl.program_id(0)
    n_used = n_used_ref[0]
    slot = i % 2

    @pl.when((i == 0) & (n_used > 0))
    def _():
        _gather_rows(x_hbm, slot_tok_ref, 0, xbuf_ref.at[0], sem_ref.at[0], MOE_ROWS)

    @pl.when(i + 1 < n_used)
    def _():
        _gather_rows(x_hbm, slot_tok_ref, (i + 1) * MOE_ROWS, xbuf_ref.at[1 - slot], sem_ref.at[1 - slot],
                     MOE_ROWS)

    @pl.when(i < n_used)
    def _():
        _wait_rows(x_hbm, xbuf_ref.at[slot], sem_ref.at[slot], MOE_ROWS)
        xb = xbuf_ref[slot].astype(BF16)
        gate = jnp.dot(xb, wg_ref[0], preferred_element_type=F32)
        up = jnp.dot(xb, wu_ref[0], preferred_element_type=F32)
        hb = (_silu(gate) * up).astype(BF16)
        y_ref[...] = jnp.dot(hb, wd_ref[0], preferred_element_type=F32)

    @pl.when(i >= n_used)
    def _():
        y_ref[...] = jnp.zeros(y_ref.shape, F32)


def moe_experts(xn, blk_expert, slot_tok, n_used, w_gate, w_up, w_down):
    t, d = xn.shape
    hidden = w_gate.shape[-1]
    n_blocks = blk_expert.shape[0]
    grid_spec = pltpu.PrefetchScalarGridSpec(
        num_scalar_prefetch=3,
        grid=(n_blocks,),
        in_specs=[pl.BlockSpec(memory_space=pl.ANY),
                  pl.BlockSpec((1, d, hidden), lambda i, be, st, nu: (be[i], 0, 0)),
                  pl.BlockSpec((1, d, hidden), lambda i, be, st, nu: (be[i], 0, 0)),
                  pl.BlockSpec((1, hidden, d), lambda i, be, st, nu: (be[i], 0, 0))],
        out_specs=pl.BlockSpec((MOE_ROWS, d), lambda i, be, st, nu: (i, 0)),
        scratch_shapes=[pltpu.VMEM((2, MOE_ROWS, d), F32),
                        pltpu.SemaphoreType.DMA((2,))],
    )
    return pl.pallas_call(
        _expert_kernel,
        grid_spec=grid_spec,
        out_shape=jax.ShapeDtypeStruct((n_blocks * MOE_ROWS, d), F32),
        compiler_params=_params("arbitrary"),
        name="moe_experts",
    )(blk_expert, slot_tok, n_used, xn, w_gate, w_up, w_down)


def _combine_kernel(dest0_ref, dest1_ref, h_ref, gate_ref, y_hbm, o_ref, buf_ref, sem_ref, *, tm):
    i = pl.program_id(0)
    n = pl.num_programs(0)
    slot = i % 2

    def gather(step, s):
        _gather_rows(y_hbm, dest0_ref, step * tm, buf_ref.at[s, 0], sem_ref.at[s], tm)
        _gather_rows(y_hbm, dest1_ref, step * tm, buf_ref.at[s, 1], sem_ref.at[s], tm)

    @pl.when(i == 0)
    def _():
        gather(0, 0)

    @pl.when(i + 1 < n)
    def _():
        gather(i + 1, 1 - slot)

    _wait_rows(y_hbm, buf_ref.at[slot, 0], sem_ref.at[slot], tm)
    _wait_rows(y_hbm, buf_ref.at[slot, 1], sem_ref.at[slot], tm)
    gates = gate_ref[...]
    o_ref[...] = h_ref[...] + (gates[:, 0:1] * buf_ref[slot, 0] + gates[:, 1:2] * buf_ref[slot, 1])


def moe_combine(h, gates, ys, dest0, dest1, *, tm=256):
    t, d = h.shape
    grid_spec = pltpu.PrefetchScalarGridSpec(
        num_scalar_prefetch=2,
        grid=(t // tm,),
        in_specs=[pl.BlockSpec((tm, d), lambda i, d0, d1: (i, 0)),
                  pl.BlockSpec((tm, LANES), lambda i, d0, d1: (i, 0)),
                  pl.BlockSpec(memory_space=pl.ANY)],
        out_specs=pl.BlockSpec((tm, d), lambda i, d0, d1: (i, 0)),
        scratch_shapes=[pltpu.VMEM((2, 2, tm, d), F32),
                        pltpu.SemaphoreType.DMA((2,))],
    )
    return pl.pallas_call(
        functools.partial(_combine_kernel, tm=tm),
        grid_spec=grid_spec,
        out_shape=jax.ShapeDtypeStruct((t, d), F32),
        compiler_params=_params("arbitrary"),
        name="moe_combine",
    )(dest0, dest1, h, gates, ys)


def hierarchical_moe(h, gain, w_rg, b_rg, w_re, b_re, w_gate, w_up, w_down):
    t, d = h.shape
    xn, gates, eids = moe_router(h, gain, w_rg, b_rg, w_re, b_re)
    eid = eids[:, :MOE_TOP_K].reshape(-1)
    n_assign = t * MOE_TOP_K
    onehot = (eid[:, None] == jnp.arange(MOE_EXPERTS, dtype=jnp.int32)[None, :]).astype(jnp.int32)
    running = jnp.cumsum(onehot, axis=0)
    rank = jnp.sum(running * onehot, axis=1) - 1
    counts = running[-1]
    padded = (counts + MOE_ROWS - 1) // MOE_ROWS * MOE_ROWS
    pad_end = jnp.cumsum(padded)
    pad_start = pad_end - padded
    dest = (pad_start[eid] + rank).astype(jnp.int32)
    n_blocks = n_assign // MOE_ROWS + MOE_EXPERTS
    tok = jnp.arange(n_assign, dtype=jnp.int32) // MOE_TOP_K
    slot_tok = jnp.zeros((n_blocks * MOE_ROWS,), jnp.int32).at[dest].set(tok)
    blk_expert = jnp.minimum(
        jnp.searchsorted(pad_end, jnp.arange(n_blocks, dtype=jnp.int32) * MOE_ROWS, side='right'),
        MOE_EXPERTS - 1).astype(jnp.int32)
    n_used = (pad_end[-1:] // MOE_ROWS).astype(jnp.int32)
    ys = moe_experts(xn, blk_expert, slot_tok, n_used, w_gate, w_up, w_down)
    dest2 = dest.reshape(t, MOE_TOP_K)
    return moe_combine(h, gates, ys, dest2[:, 0], dest2[:, 1])


def kernel(x, norm_a, w_in_a, conv_a, a_log_a, dt_bias_a, out_norm_a, w_out_a, norm_kv, w_kv, k_norm_b, norm_b, w_q_b, q_norm_b, w_out_b, norm_moe, w_router_group, b_router_group, w_router_expert, b_router_expert, w_gate, w_up, w_down):
    batch, seq, d = x.shape
    depth = norm_moe.shape[0]
    n_a = norm_a.shape[0]
    n_kheads = d // HEAD_DIM
    kdim = n_kheads * HEAD_DIM
    n_main = 4 * kdim + 2 * kdim
    h = x.reshape(batch * seq, d)
    kv = None
    for layer in range(depth):
        if layer < n_a:
            i = layer
            w_in = w_in_a[i]
            proj = norm_matmul(h, norm_a[i], w_in[:, :n_main].astype(BF16))
            gates = gdn_gates(h, norm_a[i], w_in[:, n_main:], a_log_a[i], dt_bias_a[i])
            o = gdn_core(proj, gates, conv_a[i], out_norm_a[i], batch=batch, seq=seq, n_kheads=n_kheads)
            h = matmul_residual(o, w_out_a[i].astype(BF16), h)
        else:
            i = layer - n_a
            if i == 0:
                kv = norm_matmul(h, norm_kv, w_kv.astype(BF16))
            q = norm_matmul(h, norm_b[i], w_q_b[i].astype(BF16))
            o = sb_attention(q, kv, q_norm_b[i], k_norm_b, batch=batch, seq=seq, n_heads=n_kheads)
            h = matmul_residual(o, w_out_b[i].astype(BF16), h)
        h = hierarchical_moe(h, norm_moe[layer], w_router_group[layer], b_router_group[layer],
                             w_router_expert[layer], b_router_expert[layer],
                             w_gate[layer].astype(BF16), w_up[layer].astype(BF16), w_down[layer].astype(BF16))
    return h.reshape(batch, seq, d)
```

```python
import functools

import jax
import jax.numpy as jnp
from jax import lax
from jax.experimental import pallas as pl
from jax.experimental.pallas import tpu as pltpu

NORM_EPS = 1e-6
HEAD_DIM = 128
GDN_CONV = 4
GDN_CHUNK = 64
GDN_GROUP = 8
SB_TILE = 256
MOE_GROUPS = 4
MOE_EXPERTS_PER_GROUP = 8
MOE_EXPERTS = MOE_GROUPS * MOE_EXPERTS_PER_GROUP
MOE_TOP_K = 2
MOE_ROWS = 256
LANES = 128
VMEM_LIMIT = 56 * 1024 * 1024

BF16 = jnp.bfloat16
F32 = jnp.float32


def _params(*semantics):
    return pltpu.CompilerParams(dimension_semantics=semantics, vmem_limit_bytes=VMEM_LIMIT)


def _silu(x):
    return x * (1.0 / (1.0 + jnp.exp(-x)))


def _softplus(x):
    return jnp.maximum(x, 0.0) + jnp.log(1.0 + jnp.exp(-jnp.abs(x)))


def _rms_rows(x):
    return x * lax.rsqrt(jnp.mean(x * x, axis=-1, keepdims=True) + NORM_EPS)


def _bdot(a, b):
    return jnp.einsum('nik,nkj->nij', a.astype(BF16), b.astype(BF16), preferred_element_type=F32)


def _matmul_kernel(*refs, has_scale, has_res):
    a_ref, w_ref = refs[0], refs[1]
    scale_ref = refs[2] if has_scale else None
    res_ref = refs[2 + has_scale] if has_res else None
    o_ref, wb_ref = refs[-2], refs[-1]

    @pl.when(pl.program_id(1) == 0)
    def _():
        w = w_ref[...]
        if has_scale:
            w = w * scale_ref[...]
        wb_ref[...] = w.astype(BF16)

    acc = jnp.dot(a_ref[...], wb_ref[...], preferred_element_type=F32)
    if has_res:
        acc = res_ref[...] + acc
    o_ref[...] = acc.astype(o_ref.dtype)


def matmul(a, w, *, n_cols=None, scale=None, res=None, tm, tn, out_dtype=F32):
    t, k = a.shape
    n = w.shape[1] if n_cols is None else n_cols
    tn = min(tn, n)
    in_specs = [pl.BlockSpec((tm, k), lambda j, i: (i, 0)),
                pl.BlockSpec((k, tn), lambda j, i: (0, j))]
    args = [a, w]
    if scale is not None:
        in_specs.append(pl.BlockSpec((k, 1), lambda j, i: (0, 0)))
        args.append(scale.reshape(k, 1))
    if res is not None:
        in_specs.append(pl.BlockSpec((tm, tn), lambda j, i: (i, j)))
        args.append(res)
    return pl.pallas_call(
        functools.partial(_matmul_kernel, has_scale=scale is not None, has_res=res is not None),
        grid=(n // tn, t // tm),
        in_specs=in_specs,
        out_specs=pl.BlockSpec((tm, tn), lambda j, i: (i, j)),
        out_shape=jax.ShapeDtypeStruct((t, n), out_dtype),
        scratch_shapes=[pltpu.VMEM((k, tn), BF16)],
        compiler_params=_params("arbitrary", "arbitrary"),
        name="matmul",
    )(*args)


def _gdn_gates_kernel(x_ref, g_ref, w_ref, alog_ref, dt_ref, xn_ref, o_ref, *, n_heads):
    xn = (_rms_rows(x_ref[...]) * g_ref[...]).astype(BF16)
    xn_ref[...] = xn
    y = jnp.dot(xn, w_ref[...], preferred_element_type=F32)
    lane = lax.broadcasted_iota(jnp.int32, y.shape, 1)
    row = lax.broadcasted_iota(jnp.int32, y.shape, 0)
    beta = 1.0 / (1.0 + jnp.exp(-y))
    g = -jnp.exp(alog_ref[...]) * _softplus(y + dt_ref[...])
    pos = row & (GDN_CHUNK - 1)
    shift = 1
    while shift < GDN_CHUNK:
        g = g + jnp.where(pos >= shift, pltpu.roll(g, shift, axis=0), 0.0)
        shift *= 2
    o_ref[...] = jnp.where(lane < n_heads, beta, jnp.where(lane < 2 * n_heads, g, 0.0))


def gdn_gates(x, gain, w_small, a_log, dt_bias, *, tm=512):
    t, d = x.shape
    hv = a_log.shape[0]
    w_pad = jnp.zeros((d, LANES), BF16).at[:, :2 * hv].set(w_small.astype(BF16))
    row_param = lambda p: jnp.zeros((1, LANES), F32).at[0, hv:2 * hv].set(p.astype(F32))
    return pl.pallas_call(
        functools.partial(_gdn_gates_kernel, n_heads=hv),
        grid=(t // tm,),
        in_specs=[pl.BlockSpec((tm, d), lambda i: (i, 0)),
                  pl.BlockSpec((1, d), lambda i: (0, 0)),
                  pl.BlockSpec((d, LANES), lambda i: (0, 0)),
                  pl.BlockSpec((1, LANES), lambda i: (0, 0)),
                  pl.BlockSpec((1, LANES), lambda i: (0, 0))],
        out_specs=[pl.BlockSpec((tm, d), lambda i: (i, 0)),
                   pl.BlockSpec((tm, LANES), lambda i: (i, 0))],
        out_shape=[jax.ShapeDtypeStruct((t, d), BF16),
                   jax.ShapeDtypeStruct((t, LANES), F32)],
        compiler_params=_params("arbitrary"),
        name="gdn_gates",
    )(x, gain.reshape(1, d), w_pad, row_param(a_log), row_param(dt_bias))


def _unit_lower_inverse(low):
    c = low.shape[-1]
    ri = lax.broadcasted_iota(jnp.int32, low.shape, 1)
    ci = lax.broadcasted_iota(jnp.int32, low.shape, 2)
    eye = jnp.where(ri == ci, 1.0, 0.0).astype(F32)
    x = eye
    s = 1
    while s < c:
        join = ((ri & -(2 * s)) == (ci & -(2 * s))) & ((ri & s) != 0) & ((ci & s) == 0)
        cs = jnp.where(join, low, 0.0)
        if s == 1:
            x = eye - cs
        else:
            xb = x.astype(BF16)
            x = x - _bdot(xb, _bdot(cs, xb))
        s *= 2
    return x


def _gdn_kernel(q_ref, k_ref, v_ref, z_ref, gb_ref, gct_ref, cq_ref, ck_ref, cv_ref, gain_ref, o_ref,
                pad_ref, qs_ref, ks_ref, kt_ref, vs_ref, bcol_ref, gcol_ref,
                u_ref, w_ref, a_ref, st_ref, *, n_vheads):
    seq = q_ref.shape[0]
    n_chunks = seq // GDN_CHUNK
    c = GDN_CHUNK
    grp_chunks = GDN_GROUP
    h = pl.program_id(1)

    def conv_silu(x_ref, cw_ref, width):
        pad_ref[0:8, 0:width] = jnp.zeros((8, width), F32)
        pad_ref[8:8 + seq, 0:width] = x_ref[...]
        y = pad_ref[5:5 + seq, 0:width] * cw_ref[0:1, :]
        for j in range(1, GDN_CONV):
            y = y + pad_ref[5 + j:5 + j + seq, 0:width] * cw_ref[j:j + 1, :]
        return _silu(y)

    def l2n(x):
        return x * lax.rsqrt(jnp.sum(x * x, axis=-1, keepdims=True) + NORM_EPS)

    qs_ref[...] = l2n(conv_silu(q_ref, cq_ref, HEAD_DIM)) * (HEAD_DIM ** -0.5)
    ks_ref[...] = l2n(conv_silu(k_ref, ck_ref, HEAD_DIM))
    vs_ref[...] = conv_silu(v_ref, cv_ref, 2 * HEAD_DIM)
    for ch in range(n_chunks):
        kt_ref[ch] = ks_ref[ch * c:(ch + 1) * c, :].T.astype(BF16)

    gb = gb_ref[...]
    lane = lax.broadcasted_iota(jnp.int32, gb.shape, 1)
    for r in range(2):
        hv = 2 * h + r
        bcol = jnp.sum(jnp.where(lane == hv, gb, 0.0), axis=1, keepdims=True)
        gcol = jnp.sum(jnp.where(lane == n_vheads + hv, gb, 0.0), axis=1, keepdims=True)
        bcol_ref[r] = jnp.broadcast_to(bcol, (seq, HEAD_DIM))
        gcol_ref[r] = jnp.broadcast_to(gcol, (seq, HEAD_DIM))

    def phase1(grp, carry):
        g = grp_chunks
        rows = pl.ds(pl.multiple_of(grp * (g * c), g * c), g * c)
        chunks = pl.ds(grp * g, g)
        q3 = qs_ref[rows, :].reshape(g, c, HEAD_DIM)
        k3 = ks_ref[rows, :].reshape(g, c, HEAD_DIM)
        qkk = _bdot(jnp.concatenate([q3, k3], axis=1), kt_ref[chunks])
        qk, kk = qkk[:, :c], qkk[:, c:]
        ri = lax.broadcasted_iota(jnp.int32, (g, c, c), 1)
        ci = lax.broadcasted_iota(jnp.int32, (g, c, c), 2)
        incl = ri >= ci
        lows, rhss = [], []
        for r in range(2):
            b = bcol_ref[r, rows, :].reshape(g, c, HEAD_DIM)
            gcf = gcol_ref[r, rows, :].reshape(g, c, HEAD_DIM)
            grow = gct_ref[0, r, chunks]
            diff = gcf[:, :, :c] - grow
            decay = jnp.where(incl, jnp.exp(jnp.where(incl, diff, 0.0)), 0.0)
            lows.append(jnp.where(ri > ci, b[:, :, :c] * kk * decay, 0.0))
            a_ref[r, rows, :] = (qk * decay).reshape(g * c, c).astype(BF16)
            v3 = vs_ref[rows, r * HEAD_DIM:(r + 1) * HEAD_DIM].reshape(g, c, HEAD_DIM)
            rhss.append(jnp.concatenate([v3 * b, k3 * b * jnp.exp(gcf)], axis=2))
        tinv = _unit_lower_inverse(jnp.concatenate(lows, axis=0))
        uw = _bdot(tinv, jnp.concatenate(rhss, axis=0))
        for r in range(2):
            uw_r = uw[r * g:(r + 1) * g].reshape(g * c, 2 * HEAD_DIM)
            u_ref[r, rows, :] = uw_r[:, :HEAD_DIM]
            w_ref[r, rows, :] = uw_r[:, HEAD_DIM:].astype(BF16)
        return carry

    lax.fori_loop(0, n_chunks // grp_chunks, phase1, 0)

    st_ref[...] = jnp.zeros(st_ref.shape, F32)
    gain = gain_ref[...]

    def phase2(ch, carry):
        rows = pl.ds(pl.multiple_of(ch * c, c), c)
        qc = qs_ref[rows, :]
        gcf = gcol_ref[:, rows, :]
        grow = gct_ref[0, :, ch]
        glast = gcf[:, c - 1:c, :]
        state = st_ref[...]
        wq = jnp.concatenate([w_ref[:, rows, :], (qc[None] * jnp.exp(gcf)).astype(BF16)], axis=1)
        ws = _bdot(wq, state)
        v_new = u_ref[:, rows, :] - ws[:, :c]
        o = ws[:, c:] + _bdot(a_ref[:, rows, :], v_new)
        kdec_t = kt_ref[ch].astype(F32)[None] * jnp.exp(glast[:, :, :c] - grow)
        st_ref[...] = state * jnp.exp(glast) + _bdot(kdec_t, v_new)
        for r in range(2):
            zc = z_ref[rows, r * HEAD_DIM:(r + 1) * HEAD_DIM]
            o_ref[rows, r * HEAD_DIM:(r + 1) * HEAD_DIM] = (_rms_rows(o[r]) * gain * _silu(zc)).astype(o_ref.dtype)
        return carry

    lax.fori_loop(0, n_chunks, phase2, 0)


def gdn_core(proj, gates, conv_w, out_gain, *, batch, seq, n_kheads):
    t = proj.shape[0]
    n_vheads = 2 * n_kheads
    kdim = n_kheads * HEAD_DIM
    vdim = 2 * kdim
    n_chunks = seq // GDN_CHUNK
    c = GDN_CHUNK
    gct = gates[:, n_vheads:2 * n_vheads].reshape(batch, n_chunks, c, n_kheads, 2)
    gct = gct.transpose(0, 3, 4, 1, 2).reshape(batch * n_kheads, 2, n_chunks, 1, c)
    kq = kdim // HEAD_DIM
    return pl.pallas_call(
        functools.partial(_gdn_kernel, n_vheads=n_vheads),
        grid=(batch, n_kheads),
        in_specs=[pl.BlockSpec((seq, HEAD_DIM), lambda b, h: (b, h)),
                  pl.BlockSpec((seq, HEAD_DIM), lambda b, h: (b, kq + h)),
                  pl.BlockSpec((seq, 2 * HEAD_DIM), lambda b, h: (b, kq + h)),
                  pl.BlockSpec((seq, 2 * HEAD_DIM), lambda b, h: (b, 2 * kq + h)),
                  pl.BlockSpec((seq, LANES), lambda b, h: (b, 0)),
                  pl.BlockSpec((1, 2, n_chunks, 1, c), lambda b, h: (b * n_kheads + h, 0, 0, 0, 0)),
                  pl.BlockSpec((GDN_CONV, HEAD_DIM), lambda b, h: (0, h)),
                  pl.BlockSpec((GDN_CONV, HEAD_DIM), lambda b, h: (0, kq + h)),
                  pl.BlockSpec((GDN_CONV, 2 * HEAD_DIM), lambda b, h: (0, kq + h)),
                  pl.BlockSpec((1, HEAD_DIM), lambda b, h: (0, 0))],
        out_specs=pl.BlockSpec((seq, 2 * HEAD_DIM), lambda b, h: (b, h)),
        out_shape=jax.ShapeDtypeStruct((t, vdim), BF16),
        scratch_shapes=[pltpu.VMEM((seq + 8, 2 * HEAD_DIM), F32),
                        pltpu.VMEM((seq, HEAD_DIM), F32),
                        pltpu.VMEM((seq, HEAD_DIM), F32),
                        pltpu.VMEM((n_chunks, HEAD_DIM, c), BF16),
                        pltpu.VMEM((seq, 2 * HEAD_DIM), F32),
                        pltpu.VMEM((2, seq, HEAD_DIM), F32),
                        pltpu.VMEM((2, seq, HEAD_DIM), F32),
                        pltpu.VMEM((2, seq, HEAD_DIM), F32),
                        pltpu.VMEM((2, seq, HEAD_DIM), BF16),
                        pltpu.VMEM((2, seq, c), BF16),
                        pltpu.VMEM((2, HEAD_DIM, HEAD_DIM), F32)],
        compiler_params=_params("arbitrary", "arbitrary"),
        name="gdn_core",
    )(proj, proj, proj, proj, gates, gct, conv_w, conv_w, conv_w, out_gain.reshape(1, HEAD_DIM))


def _sb_kernel(q_ref, k_ref, v_ref, qg_ref, kg_ref, o_ref, qs_ref, ks_ref, vs_ref, tri_ref, z_ref, p_ref):
    seq = q_ref.shape[0]
    t = SB_TILE
    qs_ref[...] = (_rms_rows(q_ref[...]) * qg_ref[...] * (HEAD_DIM ** -0.5)).astype(BF16)
    ks_ref[...] = (_rms_rows(k_ref[...]) * kg_ref[...]).astype(BF16)
    vs_ref[...] = v_ref[...].astype(BF16)
    ri = lax.broadcasted_iota(jnp.int32, (t, t), 0)
    ci = lax.broadcasted_iota(jnp.int32, (t, t), 1)
    tri = jnp.where(ri > ci, 1.0, 0.0).astype(BF16)
    tri_ref[0:t, :] = tri
    tri_ref[t:2 * t, :] = tri
    causal = ci < ri

    for qi in range(seq // t):
        width = (qi + 1) * t
        z_ref[:, 0:width] = lax.dot_general(qs_ref[qi * t:(qi + 1) * t, :], ks_ref[0:width, :],
                                            (((1,), (1,)), ((), ())), preferred_element_type=F32)
        later0 = jnp.zeros((t, 1), F32)
        for s in range(qi, -1, -1):
            z = z_ref[:, s * t:(s + 1) * t]
            lf = -_softplus(z)
            if s == qi:
                lf = jnp.where(causal, lf, 0.0)
            lf_hi = lf.astype(BF16)
            lf_lo = (lf - lf_hi.astype(F32)).astype(BF16)
            later = jnp.dot(jnp.concatenate([lf_hi, lf_lo], axis=1), tri_ref[...],
                            preferred_element_type=F32) + later0
            wgt = jnp.exp(z + lf + later)
            if s == qi:
                wgt = jnp.where(causal, wgt, 0.0)
            p_ref[:, s * t:(s + 1) * t] = wgt.astype(BF16)
            later0 = later0 + jnp.sum(lf, axis=1, keepdims=True)
        o_ref[qi * t:(qi + 1) * t, :] = jnp.dot(p_ref[:, 0:width], vs_ref[0:width, :],
                                                preferred_element_type=F32).astype(o_ref.dtype)


def sb_attention(q_raw, kv_raw, q_gain, k_gain, *, batch, seq, n_heads):
    t = q_raw.shape[0]
    return pl.pallas_call(
        _sb_kernel,
        grid=(batch, n_heads),
        in_specs=[pl.BlockSpec((seq, HEAD_DIM), lambda b, h: (b, h)),
                  pl.BlockSpec((seq, HEAD_DIM), lambda b, h: (b, h)),
                  pl.BlockSpec((seq, HEAD_DIM), lambda b, h: (b, n_heads + h)),
                  pl.BlockSpec((1, HEAD_DIM), lambda b, h: (0, 0)),
                  pl.BlockSpec((1, HEAD_DIM), lambda b, h: (0, 0))],
        out_specs=pl.BlockSpec((seq, HEAD_DIM), lambda b, h: (b, h)),
        out_shape=jax.ShapeDtypeStruct((t, n_heads * HEAD_DIM), BF16),
        scratch_shapes=[pltpu.VMEM((seq, HEAD_DIM), BF16),
                        pltpu.VMEM((seq, HEAD_DIM), BF16),
                        pltpu.VMEM((seq, HEAD_DIM), BF16),
                        pltpu.VMEM((2 * SB_TILE, SB_TILE), BF16),
                        pltpu.VMEM((SB_TILE, seq), F32),
                        pltpu.VMEM((SB_TILE, seq), BF16)],
        compiler_params=_params("arbitrary", "arbitrary"),
        name="sb_attention",
    )(q_raw, kv_raw, kv_raw, q_gain.reshape(1, HEAD_DIM), k_gain.reshape(1, HEAD_DIM))


def _router_kernel(x_ref, g_ref, w_ref, b_ref, xn_ref, gate_ref, eid_ref):
    xn = _rms_rows(x_ref[...]) * g_ref[...]
    xn_ref[...] = xn
    x_hi = xn.astype(BF16)
    x_lo = (xn - x_hi.astype(F32)).astype(BF16)
    logits = jnp.dot(jnp.concatenate([x_hi, x_lo, x_hi], axis=1), w_ref[...],
                     preferred_element_type=F32) + b_ref[...]
    lane = lax.broadcasted_iota(jnp.int32, logits.shape, 1).astype(F32)
    neg = -jnp.inf
    first_lane = lambda hit: jnp.min(jnp.where(hit, lane, float(LANES)), axis=1, keepdims=True)
    lg = jnp.where(lane < MOE_GROUPS, logits, neg)
    mg = jnp.max(lg, axis=1, keepdims=True)
    g_val = 1.0 / jnp.sum(jnp.exp(lg - mg), axis=1, keepdims=True)
    g_idx = first_lane(lg == mg)
    lo = MOE_GROUPS + MOE_EXPERTS_PER_GROUP * g_idx
    le = jnp.where((lane >= lo) & (lane < lo + MOE_EXPERTS_PER_GROUP), logits, neg)
    m1 = jnp.max(le, axis=1, keepdims=True)
    i1 = first_lane(le == m1)
    le2 = jnp.where(lane == i1, neg, le)
    m2 = jnp.max(le2, axis=1, keepdims=True)
    i2 = first_lane(le2 == m2)
    den = jnp.sum(jnp.exp(le - m1), axis=1, keepdims=True)
    e1 = 1.0 / den
    e2 = jnp.exp(m2 - m1) / den
    w1 = g_val * e1 / (e1 + e2)
    w2 = g_val * e2 / (e1 + e2)
    gate_ref[...] = jnp.where(lane == 0, w1, jnp.where(lane == 1, w2, 0.0))
    eid = jnp.where(lane == 0, i1 - MOE_GROUPS, jnp.where(lane == 1, i2 - MOE_GROUPS, 0.0))
    eid_ref[...] = eid.astype(jnp.int32)


def moe_router(h, gain, w_rg, b_rg, w_re, b_re, *, tm=512):
    t, d = h.shape
    n_logits = MOE_GROUPS + MOE_EXPERTS
    w = jnp.zeros((d, LANES), F32).at[:, :MOE_GROUPS].set(w_rg).at[:, MOE_GROUPS:n_logits].set(w_re)
    w_hi = w.astype(BF16)
    w_lo = (w - w_hi.astype(F32)).astype(BF16)
    w3 = jnp.concatenate([w_hi, w_hi, w_lo], axis=0)
    bias = jnp.zeros((1, LANES), F32).at[0, :MOE_GROUPS].set(b_rg).at[0, MOE_GROUPS:n_logits].set(b_re)
    return pl.pallas_call(
        _router_kernel,
        grid=(t // tm,),
        in_specs=[pl.BlockSpec((tm, d), lambda i: (i, 0)),
                  pl.BlockSpec((1, d), lambda i: (0, 0)),
                  pl.BlockSpec((3 * d, LANES), lambda i: (0, 0)),
                  pl.BlockSpec((1, LANES), lambda i: (0, 0))],
        out_specs=[pl.BlockSpec((tm, d), lambda i: (i, 0)),
                   pl.BlockSpec((tm, LANES), lambda i: (i, 0)),
                   pl.BlockSpec((tm, LANES), lambda i: (i, 0))],
        out_shape=[jax.ShapeDtypeStruct((t, d), F32),
                   jax.ShapeDtypeStruct((t, LANES), F32),
                   jax.ShapeDtypeStruct((t, LANES), jnp.int32)],
        compiler_params=_params("arbitrary"),
        name="moe_router",
    )(h, gain.reshape(1, d), w3, bias)


def _gather_rows(src_hbm, idx_ref, base, dst_ref, sem, n_rows):
    def issue(r, carry):
        pltpu.make_async_copy(src_hbm.at[pl.ds(idx_ref[base + r], 1), :],
                              dst_ref.at[pl.ds(r, 1), :], sem).start()
        return carry
    lax.fori_loop(0, n_rows, issue, 0, unroll=8)


def _wait_rows(src_hbm, dst_ref, sem, n_rows):
    pltpu.make_async_copy(src_hbm.at[pl.ds(0, n_rows), :], dst_ref, sem).wait()


def _expert_kernel(blk_expert_ref, first_ref, next_ref, slot_tok_ref, n_used_ref,
                   x_hbm, wg_hbm, wu_hbm, wd_hbm, y_ref,
                   sg_ref, su_ref, sd_ref, wg_ref, wu_ref, wd_ref, xbuf_ref, wsem_ref, sem_ref, *, layer):
    i = pl.program_id(0)
    n_used = n_used_ref[0]
    slot = i % 2

    def weight_copies(e):
        return (pltpu.make_async_copy(wg_hbm.at[layer, e], sg_ref, wsem_ref.at[0]),
                pltpu.make_async_copy(wu_hbm.at[layer, e], su_ref, wsem_ref.at[1]),
                pltpu.make_async_copy(wd_hbm.at[layer, e], sd_ref, wsem_ref.at[2]))

    @pl.when((i == 0) & (n_used > 0))
    def _():
        for cp in weight_copies(blk_expert_ref[0]):
            cp.start()
        _gather_rows(x_hbm, slot_tok_ref, 0, xbuf_ref.at[0], sem_ref.at[0], MOE_ROWS)

    @pl.when((i < n_used) & (first_ref[i] == 1))
    def _():
        for cp in weight_copies(blk_expert_ref[i]):
            cp.wait()
        wg_ref[...] = sg_ref[...].astype(BF16)
        wu_ref[...] = su_ref[...].astype(BF16)
        wd_ref[...] = sd_ref[...].astype(BF16)

        @pl.when(next_ref[i] >= 0)
        def _():
            for cp in weight_copies(next_ref[i]):
                cp.start()

    @pl.when(i + 1 < n_used)
    def _():
        _gather_rows(x_hbm, slot_tok_ref, (i + 1) * MOE_ROWS, xbuf_ref.at[1 - slot], sem_ref.at[1 - slot],
                     MOE_ROWS)

    @pl.when(i < n_used)
    def _():
        _wait_rows(x_hbm, xbuf_ref.at[slot], sem_ref.at[slot], MOE_ROWS)
        xb = xbuf_ref[slot].astype(BF16)
        gate = jnp.dot(xb, wg_ref[...], preferred_element_type=F32)
        up = jnp.dot(xb, wu_ref[...], preferred_element_type=F32)
        hb = (_silu(gate) * up).astype(BF16)
        y_ref[...] = jnp.dot(hb, wd_ref[...], preferred_element_type=F32)

    @pl.when(i >= n_used)
    def _():
        y_ref[...] = jnp.zeros(y_ref.shape, F32)


def moe_experts(xn, blk_expert, first, nxt, slot_tok, n_used, w_gate, w_up, w_down, *, layer):
    t, d = xn.shape
    hidden = w_gate.shape[-1]
    n_blocks = blk_expert.shape[0]
    grid_spec = pltpu.PrefetchScalarGridSpec(
        num_scalar_prefetch=5,
        grid=(n_blocks,),
        in_specs=[pl.BlockSpec(memory_space=pl.ANY)] * 4,
        out_specs=pl.BlockSpec((MOE_ROWS, d), lambda i, *_: (i, 0)),
        scratch_shapes=[pltpu.VMEM((d, hidden), F32),
                        pltpu.VMEM((d, hidden), F32),
                        pltpu.VMEM((hidden, d), F32),
                        pltpu.VMEM((d, hidden), BF16),
                        pltpu.VMEM((d, hidden), BF16),
                        pltpu.VMEM((hidden, d), BF16),
                        pltpu.VMEM((2, MOE_ROWS, d), F32),
                        pltpu.SemaphoreType.DMA((3,)),
                        pltpu.SemaphoreType.DMA((2,))],
    )
    return pl.pallas_call(
        functools.partial(_expert_kernel, layer=layer),
        grid_spec=grid_spec,
        out_shape=jax.ShapeDtypeStruct((n_blocks * MOE_ROWS, d), F32),
        compiler_params=_params("arbitrary"),
        name="moe_experts",
    )(blk_expert, first, nxt, slot_tok, n_used, xn, w_gate, w_up, w_down)


def _combine_kernel(dest0_ref, dest1_ref, h_ref, gate_ref, y_hbm, *rest, tm, emit_norm):
    o_ref = rest[0]
    buf_ref, sem_ref = rest[-2], rest[-1]
    i = pl.program_id(0)
    n = pl.num_programs(0)
    slot = i % 2

    def gather(step, s):
        _gather_rows(y_hbm, dest0_ref, step * tm, buf_ref.at[s, 0], sem_ref.at[s], tm)
        _gather_rows(y_hbm, dest1_ref, step * tm, buf_ref.at[s, 1], sem_ref.at[s], tm)

    @pl.when(i == 0)
    def _():
        gather(0, 0)

    @pl.when(i + 1 < n)
    def _():
        gather(i + 1, 1 - slot)

    _wait_rows(y_hbm, buf_ref.at[slot, 0], sem_ref.at[slot], tm)
    _wait_rows(y_hbm, buf_ref.at[slot, 1], sem_ref.at[slot], tm)
    gates = gate_ref[...]
    out = h_ref[...] + (gates[:, 0:1] * buf_ref[slot, 0] + gates[:, 1:2] * buf_ref[slot, 1])
    o_ref[...] = out
    if emit_norm:
        rest[1][...] = _rms_rows(out).astype(BF16)


def moe_combine(h, gates, ys, dest0, dest1, *, emit_norm, tm=256):
    t, d = h.shape
    out_specs = [pl.BlockSpec((tm, d), lambda i, d0, d1: (i, 0))]
    out_shape = [jax.ShapeDtypeStruct((t, d), F32)]
    if emit_norm:
        out_specs.append(pl.BlockSpec((tm, d), lambda i, d0, d1: (i, 0)))
        out_shape.append(jax.ShapeDtypeStruct((t, d), BF16))
    grid_spec = pltpu.PrefetchScalarGridSpec(
        num_scalar_prefetch=2,
        grid=(t // tm,),
        in_specs=[pl.BlockSpec((tm, d), lambda i, d0, d1: (i, 0)),
                  pl.BlockSpec((tm, LANES), lambda i, d0, d1: (i, 0)),
                  pl.BlockSpec(memory_space=pl.ANY)],
        out_specs=out_specs,
        scratch_shapes=[pltpu.VMEM((2, 2, tm, d), F32),
                        pltpu.SemaphoreType.DMA((2,))],
    )
    return pl.pallas_call(
        functools.partial(_combine_kernel, tm=tm, emit_norm=emit_norm),
        grid_spec=grid_spec,
        out_shape=out_shape,
        compiler_params=_params("arbitrary"),
        name="moe_combine",
    )(dest0, dest1, h, gates, ys)


def hierarchical_moe(h, gain, w_rg, b_rg, w_re, b_re, w_gate, w_up, w_down, *, layer, emit_norm):
    t, d = h.shape
    xn, gates, eids = moe_router(h, gain, w_rg, b_rg, w_re, b_re)
    eid = eids[:, :MOE_TOP_K].reshape(-1)
    n_assign = t * MOE_TOP_K
    experts = jnp.arange(MOE_EXPERTS, dtype=jnp.int32)
    onehot = (eid[:, None] == experts[None, :]).astype(jnp.int32)
    running = jnp.cumsum(onehot, axis=0)
    rank = jnp.sum(running * onehot, axis=1) - 1
    counts = running[-1]
    padded = (counts + MOE_ROWS - 1) // MOE_ROWS * MOE_ROWS
    pad_end = jnp.cumsum(padded)
    pad_start = pad_end - padded
    dest = (pad_start[eid] + rank).astype(jnp.int32)
    n_blocks = n_assign // MOE_ROWS + MOE_EXPERTS
    tok = jnp.arange(n_assign, dtype=jnp.int32) // MOE_TOP_K
    slot_tok = jnp.zeros((n_blocks * MOE_ROWS,), jnp.int32).at[dest].set(tok)
    blk_start = jnp.arange(n_blocks, dtype=jnp.int32) * MOE_ROWS
    blk_expert = jnp.minimum(jnp.sum((pad_end[None, :] <= blk_start[:, None]).astype(jnp.int32), axis=1),
                             MOE_EXPERTS - 1).astype(jnp.int32)
    n_used = (pad_end[-1:] // MOE_ROWS).astype(jnp.int32)
    first = (blk_start == pad_start[blk_expert]).astype(jnp.int32)
    later_nonempty = (experts[None, :] > experts[:, None]) & (counts[None, :] > 0)
    next_of_expert = jnp.min(jnp.where(later_nonempty, experts[None, :], MOE_EXPERTS), axis=1)
    next_of_expert = jnp.where(next_of_expert == MOE_EXPERTS, -1, next_of_expert).astype(jnp.int32)
    nxt = next_of_expert[blk_expert]
    ys = moe_experts(xn, blk_expert, first, nxt, slot_tok, n_used, w_gate, w_up, w_down, layer=layer)
    dest2 = dest.reshape(t, MOE_TOP_K)
    return moe_combine(h, gates, ys, dest2[:, 0], dest2[:, 1], emit_norm=emit_norm)


def kernel(x, norm_a, w_in_a, conv_a, a_log_a, dt_bias_a, out_norm_a, w_out_a, norm_kv, w_kv, k_norm_b, norm_b, w_q_b, q_norm_b, w_out_b, norm_moe, w_router_group, b_router_group, w_router_expert, b_router_expert, w_gate, w_up, w_down):
    batch, seq, d = x.shape
    depth = norm_moe.shape[0]
    n_a = norm_a.shape[0]
    n_kheads = d // HEAD_DIM
    kdim = n_kheads * HEAD_DIM
    n_main = 6 * kdim
    h = x.reshape(batch * seq, d)
    hn = None
    kv = None
    for layer in range(depth):
        if layer < n_a:
            i = layer
            xn, gates = gdn_gates(h, norm_a[i], w_in_a[i][:, n_main:], a_log_a[i], dt_bias_a[i])
            proj = matmul(xn, w_in_a[i], n_cols=n_main, tm=1024, tn=1024)
            o = gdn_core(proj, gates, conv_a[i], out_norm_a[i], batch=batch, seq=seq, n_kheads=n_kheads)
            h = matmul(o, w_out_a[i], res=h, tm=512, tn=512)
        else:
            i = layer - n_a
            if i == 0:
                kv = matmul(hn, w_kv, scale=norm_kv, tm=1024, tn=1024)
            q = matmul(hn, w_q_b[i], scale=norm_b[i], tm=1024, tn=1024)
            o = sb_attention(q, kv, q_norm_b[i], k_norm_b, batch=batch, seq=seq, n_heads=n_kheads)
            h = matmul(o, w_out_b[i], res=h, tm=1024, tn=1024)
        emit_norm = n_a <= layer + 1 < depth
        out = hierarchical_moe(h, norm_moe[layer], w_router_group[layer], b_router_group[layer],
                               w_router_expert[layer], b_router_expert[layer],
                               w_gate, w_up, w_down, layer=layer, emit_norm=emit_norm)
        h, hn = (out[0], out[1]) if emit_norm else (out[0], None)
    return h.reshape(batch, seq, d)
```

```python
import functools

import jax
import jax.numpy as jnp
from jax import lax
from jax.experimental import pallas as pl
from jax.experimental.pallas import tpu as pltpu

NORM_EPS = 1e-6
LOG2E = 1.4426950408889634
HEAD_DIM = 128
GDN_CONV = 4
GDN_CHUNK = 64
GDN_GROUP = 16
SB_TILE = 256
MOE_GROUPS = 4
MOE_EXPERTS_PER_GROUP = 8
MOE_EXPERTS = MOE_GROUPS * MOE_EXPERTS_PER_GROUP
MOE_TOP_K = 2
MOE_ROWS = 256
MOE_GATHER_DEPTH = 3
LANES = 128
VMEM_LIMIT = 56 * 1024 * 1024

BF16 = jnp.bfloat16
F32 = jnp.float32


def _params(*semantics):
    return pltpu.CompilerParams(dimension_semantics=semantics, vmem_limit_bytes=VMEM_LIMIT)


def _silu(x):
    return x * (1.0 / (1.0 + jnp.exp(-x)))


def _softplus(x):
    return jnp.maximum(x, 0.0) + jnp.log(1.0 + jnp.exp(-jnp.abs(x)))


def _rms_rows(x):
    return x * lax.rsqrt(jnp.mean(x * x, axis=-1, keepdims=True) + NORM_EPS)


def _bdot(a, b):
    return jnp.einsum('nik,nkj->nij', a.astype(BF16), b.astype(BF16), preferred_element_type=F32)


def _matmul_kernel(*refs, has_scale, has_res):
    a_ref, w_ref = refs[0], refs[1]
    scale_ref = refs[2] if has_scale else None
    res_ref = refs[2 + has_scale] if has_res else None
    o_ref, wb_ref = refs[-2], refs[-1]

    @pl.when(pl.program_id(1) == 0)
    def _():
        w = w_ref[...]
        if has_scale:
            w = w * scale_ref[...]
        wb_ref[...] = w.astype(BF16)

    acc = jnp.dot(a_ref[...], wb_ref[...], preferred_element_type=F32)
    if has_res:
        acc = res_ref[...] + acc
    o_ref[...] = acc.astype(o_ref.dtype)


def matmul(a, w, *, n_cols=None, scale=None, res=None, tm, tn, out_dtype=F32):
    t, k = a.shape
    n = w.shape[1] if n_cols is None else n_cols
    tn = min(tn, n)
    while n % tn:
        tn //= 2
    assert t % tm == 0 and tn % LANES == 0, (t, tm, n, tn)
    in_specs = [pl.BlockSpec((tm, k), lambda j, i: (i, 0)),
                pl.BlockSpec((k, tn), lambda j, i: (0, j))]
    args = [a, w]
    if scale is not None:
        in_specs.append(pl.BlockSpec((k, 1), lambda j, i: (0, 0)))
        args.append(scale.reshape(k, 1))
    if res is not None:
        in_specs.append(pl.BlockSpec((tm, tn), lambda j, i: (i, j)))
        args.append(res)
    return pl.pallas_call(
        functools.partial(_matmul_kernel, has_scale=scale is not None, has_res=res is not None),
        grid=(n // tn, t // tm),
        in_specs=in_specs,
        out_specs=pl.BlockSpec((tm, tn), lambda j, i: (i, j)),
        out_shape=jax.ShapeDtypeStruct((t, n), out_dtype),
        scratch_shapes=[pltpu.VMEM((k, tn), BF16)],
        compiler_params=_params("arbitrary", "arbitrary"),
        name="matmul",
    )(*args)


def _gdn_gates_kernel(x_ref, g_ref, w_ref, alog_ref, dt_ref, xn_ref, o_ref, *, n_heads):
    xn = (_rms_rows(x_ref[...]) * g_ref[...]).astype(BF16)
    xn_ref[...] = xn
    y = jnp.dot(xn, w_ref[...], preferred_element_type=F32)
    lane = lax.broadcasted_iota(jnp.int32, y.shape, 1)
    row = lax.broadcasted_iota(jnp.int32, y.shape, 0)
    beta = 1.0 / (1.0 + jnp.exp(-y))
    g = -jnp.exp(alog_ref[...]) * _softplus(y + dt_ref[...])
    pos = row & (GDN_CHUNK - 1)
    shift = 1
    while shift < GDN_CHUNK:
        g = g + jnp.where(pos >= shift, pltpu.roll(g, shift, axis=0), 0.0)
        shift *= 2
    o_ref[...] = jnp.where(lane < n_heads, beta, jnp.where(lane < 2 * n_heads, g, 0.0))


def gdn_gates(x, gain, w_small, a_log, dt_bias, *, tm=512):
    t, d = x.shape
    hv = a_log.shape[0]
    w_pad = jnp.zeros((d, LANES), BF16).at[:, :2 * hv].set(w_small.astype(BF16))
    row_param = lambda p: jnp.zeros((1, LANES), F32).at[0, hv:2 * hv].set(p.astype(F32))
    return pl.pallas_call(
        functools.partial(_gdn_gates_kernel, n_heads=hv),
        grid=(t // tm,),
        in_specs=[pl.BlockSpec((tm, d), lambda i: (i, 0)),
                  pl.BlockSpec((1, d), lambda i: (0, 0)),
                  pl.BlockSpec((d, LANES), lambda i: (0, 0)),
                  pl.BlockSpec((1, LANES), lambda i: (0, 0)),
                  pl.BlockSpec((1, LANES), lambda i: (0, 0))],
        out_specs=[pl.BlockSpec((tm, d), lambda i: (i, 0)),
                   pl.BlockSpec((tm, LANES), lambda i: (i, 0))],
        out_shape=[jax.ShapeDtypeStruct((t, d), BF16),
                   jax.ShapeDtypeStruct((t, LANES), F32)],
        compiler_params=_params("arbitrary"),
        name="gdn_gates",
    )(x, gain.reshape(1, d), w_pad, row_param(a_log), row_param(dt_bias))


def _unit_lower_inverse(low):
    c = low.shape[-1]
    ri = lax.broadcasted_iota(jnp.int32, low.shape, 1)
    ci = lax.broadcasted_iota(jnp.int32, low.shape, 2)
    eye = jnp.where(ri == ci, 1.0, 0.0).astype(F32)
    x = eye
    s = 1
    while s < c:
        join = ((ri & -(2 * s)) == (ci & -(2 * s))) & ((ri & s) != 0) & ((ci & s) == 0)
        cs = jnp.where(join, low, 0.0)
        if s == 1:
            x = eye - cs
        else:
            xb = x.astype(BF16)
            x = x - _bdot(xb, _bdot(cs, xb))
        s *= 2
    return x


def _gdn_kernel(q_ref, k_ref, v_ref, z_ref, gb_ref, gct_ref, cq_ref, ck_ref, cv_ref, gain_ref, o_ref,
                pad_ref, qs_ref, ks_ref, kt_ref, vs_ref, bcol_ref, gcol_ref,
                pq_ref, n_ref, op_ref, d_ref, *, n_vheads):
    seq = q_ref.shape[0]
    n_chunks = seq // GDN_CHUNK
    c = GDN_CHUNK
    grp_chunks = GDN_GROUP
    h = pl.program_id(1)

    grp_rows = grp_chunks * c
    pad_ref[0:8, :] = jnp.zeros((8, pad_ref.shape[1]), F32)

    def l2n(x):
        return x * lax.rsqrt(jnp.sum(x * x, axis=-1, keepdims=True) + NORM_EPS)

    def prep(grp):
        r0 = grp * grp_rows
        sl = slice(r0, r0 + grp_rows)

        def conv_silu(x_ref, cw_ref, col0, width):
            cols = slice(col0, col0 + width)
            pad_ref[8 + r0:8 + r0 + grp_rows, cols] = x_ref[sl, :]
            y = pad_ref[5 + r0:5 + r0 + grp_rows, cols] * cw_ref[0:1, :]
            for j in range(1, GDN_CONV):
                y = y + pad_ref[5 + j + r0:5 + j + r0 + grp_rows, cols] * cw_ref[j:j + 1, :]
            return _silu(y)

        qs_ref[sl, :] = l2n(conv_silu(q_ref, cq_ref, 0, HEAD_DIM)) * (HEAD_DIM ** -0.5)
        kk_ = l2n(conv_silu(k_ref, ck_ref, HEAD_DIM, HEAD_DIM))
        ks_ref[sl, :] = kk_
        vs_ref[sl, :] = conv_silu(v_ref, cv_ref, 2 * HEAD_DIM, 2 * HEAD_DIM)
        for ci_ in range(grp_chunks):
            kt_ref[grp * grp_chunks + ci_] = kk_[ci_ * c:(ci_ + 1) * c, :].T.astype(BF16)
        gb = gb_ref[sl, :]
        lane = lax.broadcasted_iota(jnp.int32, gb.shape, 1)
        for r in range(2):
            hv = 2 * h + r
            bcol = jnp.sum(jnp.where(lane == hv, gb, 0.0), axis=1, keepdims=True)
            gcol = jnp.sum(jnp.where(lane == n_vheads + hv, gb, 0.0), axis=1, keepdims=True)
            bcol_ref[r, sl, :] = jnp.broadcast_to(bcol, (grp_rows, HEAD_DIM))
            gcol_ref[r, sl, :] = jnp.broadcast_to(gcol, (grp_rows, HEAD_DIM))

    def phase1(grp):
        g = grp_chunks
        rows = slice(grp * grp_rows, (grp + 1) * grp_rows)
        chunks = slice(grp * g, (grp + 1) * g)
        q3 = qs_ref[rows, :].reshape(g, c, HEAD_DIM)
        k3 = ks_ref[rows, :].reshape(g, c, HEAD_DIM)
        kt = kt_ref[chunks]
        qkk = _bdot(jnp.concatenate([q3, k3], axis=1), kt)
        qk, kk = qkk[:, :c], qkk[:, c:]
        ri = lax.broadcasted_iota(jnp.int32, (g, c, c), 1)
        ci = lax.broadcasted_iota(jnp.int32, (g, c, c), 2)
        incl = ri >= ci
        lows, rhss, attns, kdts, qgs = [], [], [], [], []
        for r in range(2):
            b = bcol_ref[r, rows, :].reshape(g, c, HEAD_DIM)
            gcf = gcol_ref[r, rows, :].reshape(g, c, HEAD_DIM)
            grow = gct_ref[0, r, chunks]
            glast = gcf[:, c - 1:c, :]
            diff = gcf[:, :, :c] - grow
            decay = jnp.where(incl, jnp.exp(jnp.where(incl, diff, 0.0)), 0.0)
            lows.append(jnp.where(ri > ci, b[:, :, :c] * kk * decay, 0.0))
            attns.append(qk * decay)
            egc = jnp.exp(gcf)
            v3 = vs_ref[rows, r * HEAD_DIM:(r + 1) * HEAD_DIM].reshape(g, c, HEAD_DIM)
            rhss.append(jnp.concatenate([v3 * b, k3 * b * egc], axis=2))
            kdts.append(kt.astype(F32) * jnp.exp(glast[:, :, :c] - grow))
            qgs.append(q3 * egc)
            d_ref[r, chunks] = jnp.exp(glast)
        tinv = _unit_lower_inverse(jnp.concatenate(lows, axis=0))
        uw = _bdot(tinv, jnp.concatenate(rhss, axis=0)).astype(BF16)
        auw = _bdot(jnp.concatenate(attns, axis=0), uw)
        kuw = _bdot(jnp.concatenate(kdts, axis=0), uw)
        for r in range(2):
            sel = slice(r * g, (r + 1) * g)
            pq_ref[r, chunks] = jnp.concatenate(
                [kuw[sel, :, HEAD_DIM:], qgs[r] - auw[sel, :, HEAD_DIM:]], axis=1).astype(BF16)
            n_ref[r, chunks] = kuw[sel, :, :HEAD_DIM]
            op_ref[r, rows, :] = auw[sel, :, :HEAD_DIM].reshape(g * c, HEAD_DIM)

    for grp in range(n_chunks // grp_chunks):
        prep(grp)
        phase1(grp)

    gain = gain_ref[...]
    state = jnp.zeros((2, HEAD_DIM, HEAD_DIM), F32)
    for ch in range(n_chunks):
        rows = slice(ch * c, (ch + 1) * c)
        ps = _bdot(pq_ref[:, ch], state)
        state = state * d_ref[:, ch] - ps[:, :HEAD_DIM] + n_ref[:, ch]
        o = op_ref[:, rows, :] + ps[:, HEAD_DIM:]
        for r in range(2):
            zc = z_ref[rows, r * HEAD_DIM:(r + 1) * HEAD_DIM]
            o_ref[rows, r * HEAD_DIM:(r + 1) * HEAD_DIM] = (_rms_rows(o[r]) * gain * _silu(zc)).astype(o_ref.dtype)


def gdn_core(proj, gates, conv_w, out_gain, *, batch, seq, n_kheads):
    t = proj.shape[0]
    n_vheads = 2 * n_kheads
    kdim = n_kheads * HEAD_DIM
    vdim = 2 * kdim
    n_chunks = seq // GDN_CHUNK
    c = GDN_CHUNK
    gct = gates[:, n_vheads:2 * n_vheads].reshape(batch, n_chunks, c, n_kheads, 2)
    gct = gct.transpose(0, 3, 4, 1, 2).reshape(batch * n_kheads, 2, n_chunks, 1, c)
    kq = kdim // HEAD_DIM
    return pl.pallas_call(
        functools.partial(_gdn_kernel, n_vheads=n_vheads),
        grid=(batch, n_kheads),
        in_specs=[pl.BlockSpec((seq, HEAD_DIM), lambda b, h: (b, h)),
                  pl.BlockSpec((seq, HEAD_DIM), lambda b, h: (b, kq + h)),
                  pl.BlockSpec((seq, 2 * HEAD_DIM), lambda b, h: (b, kq + h)),
                  pl.BlockSpec((seq, 2 * HEAD_DIM), lambda b, h: (b, 2 * kq + h)),
                  pl.BlockSpec((seq, LANES), lambda b, h: (b, 0)),
                  pl.BlockSpec((1, 2, n_chunks, 1, c), lambda b, h: (b * n_kheads + h, 0, 0, 0, 0)),
                  pl.BlockSpec((GDN_CONV, HEAD_DIM), lambda b, h: (0, h)),
                  pl.BlockSpec((GDN_CONV, HEAD_DIM), lambda b, h: (0, kq + h)),
                  pl.BlockSpec((GDN_CONV, 2 * HEAD_DIM), lambda b, h: (0, kq + h)),
                  pl.BlockSpec((1, HEAD_DIM), lambda b, h: (0, 0))],
        out_specs=pl.BlockSpec((seq, 2 * HEAD_DIM), lambda b, h: (b, h)),
        out_shape=jax.ShapeDtypeStruct((t, vdim), BF16),
        scratch_shapes=[pltpu.VMEM((seq + 8, 4 * HEAD_DIM), F32),
                        pltpu.VMEM((seq, HEAD_DIM), F32),
                        pltpu.VMEM((seq, HEAD_DIM), F32),
                        pltpu.VMEM((n_chunks, HEAD_DIM, c), BF16),
                        pltpu.VMEM((seq, 2 * HEAD_DIM), F32),
                        pltpu.VMEM((2, seq, HEAD_DIM), F32),
                        pltpu.VMEM((2, seq, HEAD_DIM), F32),
                        pltpu.VMEM((2, n_chunks, HEAD_DIM + c, HEAD_DIM), BF16),
                        pltpu.VMEM((2, n_chunks, HEAD_DIM, HEAD_DIM), F32),
                        pltpu.VMEM((2, seq, HEAD_DIM), F32),
                        pltpu.VMEM((2, n_chunks, 1, HEAD_DIM), F32)],
        compiler_params=_params("arbitrary", "arbitrary"),
        name="gdn_core",
    )(proj, proj, proj, proj, gates, gct, conv_w, conv_w, conv_w, out_gain.reshape(1, HEAD_DIM))


def _sb_kernel(q_ref, k_ref, v_ref, qg_ref, kg_ref, o_ref, qs_ref, ks_ref, vs_ref, tri_ref, z_ref, p_ref):
    seq = q_ref.shape[0]
    t = SB_TILE
    qs_ref[...] = (_rms_rows(q_ref[...]) * qg_ref[...] * (-LOG2E * HEAD_DIM ** -0.5)).astype(BF16)
    ks_ref[...] = (_rms_rows(k_ref[...]) * kg_ref[...]).astype(BF16)
    vs_ref[...] = v_ref[...].astype(BF16)
    ri = lax.broadcasted_iota(jnp.int32, (t, t), 0)
    ci = lax.broadcasted_iota(jnp.int32, (t, t), 1)
    tri = jnp.where(ri > ci, 1.0, 0.0).astype(BF16)
    tri_ref[0:t, :] = tri
    tri_ref[t:2 * t, :] = tri
    causal = ci < ri

    for qi in range(seq // t):
        width = (qi + 1) * t
        z_ref[:, 0:width] = lax.dot_general(qs_ref[qi * t:(qi + 1) * t, :], ks_ref[0:width, :],
                                            (((1,), (1,)), ((), ())), preferred_element_type=F32)
        later0 = jnp.zeros((t, 1), F32)
        for s in range(qi, -1, -1):
            zn = z_ref[:, s * t:(s + 1) * t]
            lf = jnp.minimum(zn, 0.0) - jnp.log(1.0 + jnp.exp2(-jnp.abs(zn))) * LOG2E
            if s == qi:
                lf = jnp.where(causal, lf, 0.0)
            lf_hi = lf.astype(BF16)
            lf_lo = (lf - lf_hi.astype(F32)).astype(BF16)
            later = jnp.dot(jnp.concatenate([lf_hi, lf_lo], axis=1), tri_ref[...],
                            preferred_element_type=F32) + later0
            wgt = jnp.exp2((lf - zn) + later)
            if s == qi:
                wgt = jnp.where(causal, wgt, 0.0)
            p_ref[:, s * t:(s + 1) * t] = wgt.astype(BF16)
            later0 = later0 + jnp.sum(lf, axis=1, keepdims=True)
        o_ref[qi * t:(qi + 1) * t, :] = jnp.dot(p_ref[:, 0:width], vs_ref[0:width, :],
                                                preferred_element_type=F32).astype(o_ref.dtype)


def sb_attention(q_raw, kv_raw, q_gain, k_gain, *, batch, seq, n_heads):
    t = q_raw.shape[0]
    return pl.pallas_call(
        _sb_kernel,
        grid=(batch, n_heads),
        in_specs=[pl.BlockSpec((seq, HEAD_DIM), lambda b, h: (b, h)),
                  pl.BlockSpec((seq, HEAD_DIM), lambda b, h: (b, h)),
                  pl.BlockSpec((seq, HEAD_DIM), lambda b, h: (b, n_heads + h)),
                  pl.BlockSpec((1, HEAD_DIM), lambda b, h: (0, 0)),
                  pl.BlockSpec((1, HEAD_DIM), lambda b, h: (0, 0))],
        out_specs=pl.BlockSpec((seq, HEAD_DIM), lambda b, h: (b, h)),
        out_shape=jax.ShapeDtypeStruct((t, n_heads * HEAD_DIM), BF16),
        scratch_shapes=[pltpu.VMEM((seq, HEAD_DIM), BF16),
                        pltpu.VMEM((seq, HEAD_DIM), BF16),
                        pltpu.VMEM((seq, HEAD_DIM), BF16),
                        pltpu.VMEM((2 * SB_TILE, SB_TILE), BF16),
                        pltpu.VMEM((SB_TILE, seq), F32),
                        pltpu.VMEM((SB_TILE, seq), BF16)],
        compiler_params=_params("arbitrary", "arbitrary"),
        name="sb_attention",
    )(q_raw, kv_raw, kv_raw, q_gain.reshape(1, HEAD_DIM), k_gain.reshape(1, HEAD_DIM))


def _router_kernel(x_ref, g_ref, w_ref, b_ref, xn_ref, gate_ref, eid_ref):
    xn = _rms_rows(x_ref[...]) * g_ref[...]
    xn_ref[...] = xn
    x_hi = xn.astype(BF16)
    x_lo = (xn - x_hi.astype(F32)).astype(BF16)
    logits = jnp.dot(jnp.concatenate([x_hi, x_lo, x_hi], axis=1), w_ref[...],
                     preferred_element_type=F32) + b_ref[...]
    lane = lax.broadcasted_iota(jnp.int32, logits.shape, 1).astype(F32)
    neg = -jnp.inf
    first_lane = lambda hit: jnp.min(jnp.where(hit, lane, float(LANES)), axis=1, keepdims=True)
    lg = jnp.where(lane < MOE_GROUPS, logits, neg)
    mg = jnp.max(lg, axis=1, keepdims=True)
    g_val = 1.0 / jnp.sum(jnp.exp(lg - mg), axis=1, keepdims=True)
    g_idx = first_lane(lg == mg)
    lo = MOE_GROUPS + MOE_EXPERTS_PER_GROUP * g_idx
    le = jnp.where((lane >= lo) & (lane < lo + MOE_EXPERTS_PER_GROUP), logits, neg)
    m1 = jnp.max(le, axis=1, keepdims=True)
    i1 = first_lane(le == m1)
    le2 = jnp.where(lane == i1, neg, le)
    m2 = jnp.max(le2, axis=1, keepdims=True)
    i2 = first_lane(le2 == m2)
    den = jnp.sum(jnp.exp(le - m1), axis=1, keepdims=True)
    e1 = 1.0 / den
    e2 = jnp.exp(m2 - m1) / den
    w1 = g_val * e1 / (e1 + e2)
    w2 = g_val * e2 / (e1 + e2)
    gate_ref[...] = jnp.where(lane == 0, w1, jnp.where(lane == 1, w2, 0.0))
    eid = jnp.where(lane == 0, i1 - MOE_GROUPS, jnp.where(lane == 1, i2 - MOE_GROUPS, 0.0))
    eid_ref[...] = eid.astype(jnp.int32)


def moe_router(h, gain, w_rg, b_rg, w_re, b_re, *, tm=512):
    t, d = h.shape
    n_logits = MOE_GROUPS + MOE_EXPERTS
    w = jnp.zeros((d, LANES), F32).at[:, :MOE_GROUPS].set(w_rg).at[:, MOE_GROUPS:n_logits].set(w_re)
    w_hi = w.astype(BF16)
    w_lo = (w - w_hi.astype(F32)).astype(BF16)
    w3 = jnp.concatenate([w_hi, w_hi, w_lo], axis=0)
    bias = jnp.zeros((1, LANES), F32).at[0, :MOE_GROUPS].set(b_rg).at[0, MOE_GROUPS:n_logits].set(b_re)
    return pl.pallas_call(
        _router_kernel,
        grid=(t // tm,),
        in_specs=[pl.BlockSpec((tm, d), lambda i: (i, 0)),
                  pl.BlockSpec((1, d), lambda i: (0, 0)),
                  pl.BlockSpec((3 * d, LANES), lambda i: (0, 0)),
                  pl.BlockSpec((1, LANES), lambda i: (0, 0))],
        out_specs=[pl.BlockSpec((tm, d), lambda i: (i, 0)),
                   pl.BlockSpec((tm, LANES), lambda i: (i, 0)),
                   pl.BlockSpec((tm, LANES), lambda i: (i, 0))],
        out_shape=[jax.ShapeDtypeStruct((t, d), F32),
                   jax.ShapeDtypeStruct((t, LANES), F32),
                   jax.ShapeDtypeStruct((t, LANES), jnp.int32)],
        compiler_params=_params("arbitrary"),
        name="moe_router",
    )(h, gain.reshape(1, d), w3, bias)


def _gather_rows(src_hbm, idx_ref, base, dst_ref, sem, n_rows, unrolled=False):
    def issue(r, carry):
        pltpu.make_async_copy(src_hbm.at[pl.ds(idx_ref[base + r], 1), :],
                              dst_ref.at[pl.ds(r, 1), :], sem).start()
        return carry
    if unrolled:
        for r in range(n_rows):
            issue(r, 0)
    else:
        lax.fori_loop(0, n_rows, issue, 0, unroll=8)


def _wait_rows(src_hbm, dst_ref, sem, n_rows):
    pltpu.make_async_copy(src_hbm.at[pl.ds(0, n_rows), :], dst_ref, sem).wait()


def _expert_kernel(blk_expert_ref, first_ref, next_ref, slot_tok_ref, n_used_ref,
                   x_hbm, wg_hbm, wu_hbm, wd_hbm, y_ref,
                   sg_ref, su_ref, sd_ref, wg_ref, wu_ref, wd_ref, xbuf_ref, wsem_ref, sem_ref, *, layer):
    i = pl.program_id(0)
    n_used = n_used_ref[0]
    depth = xbuf_ref.shape[0]
    slot = i % depth

    def weight_copies(e):
        return (pltpu.make_async_copy(wg_hbm.at[layer, e], sg_ref, wsem_ref.at[0]),
                pltpu.make_async_copy(wu_hbm.at[layer, e], su_ref, wsem_ref.at[1]),
                pltpu.make_async_copy(wd_hbm.at[layer, e], sd_ref, wsem_ref.at[2]))

    def gather(blk, s, unrolled):
        _gather_rows(x_hbm, slot_tok_ref, blk * MOE_ROWS, xbuf_ref.at[s], sem_ref.at[s], MOE_ROWS, unrolled)

    @pl.when((i == 0) & (n_used > 0))
    def _():
        for cp in weight_copies(blk_expert_ref[0]):
            cp.start()
        for blk in range(depth - 1):
            gather(blk, blk, False)

    is_first = (i < n_used) & (first_ref[i] == 1)

    @pl.when(is_first)
    def _():
        for cp in weight_copies(blk_expert_ref[i]):
            cp.wait()
        wg_ref[...] = sg_ref[...].astype(BF16)
        wu_ref[...] = su_ref[...].astype(BF16)
        wd_ref[...] = sd_ref[...].astype(BF16)

    @pl.when(is_first & (next_ref[i] >= 0))
    def _():
        for cp in weight_copies(next_ref[i]):
            cp.start(priority=1)

    @pl.when(i < n_used)
    def _():
        _wait_rows(x_hbm, xbuf_ref.at[slot], sem_ref.at[slot], MOE_ROWS)
        xb = xbuf_ref[slot].astype(BF16)
        gather(i + depth - 1, (i + depth - 1) % depth, True)
        gate = jnp.dot(xb, wg_ref[...], preferred_element_type=F32)
        up = jnp.dot(xb, wu_ref[...], preferred_element_type=F32)
        hb = (_silu(gate) * up).astype(BF16)
        y_ref[...] = jnp.dot(hb, wd_ref[...], preferred_element_type=F32)

    @pl.when((i >= n_used) & (i < n_used + depth - 1) & (n_used > 0))
    def _():
        _wait_rows(x_hbm, xbuf_ref.at[slot], sem_ref.at[slot], MOE_ROWS)

    @pl.when(i >= n_used)
    def _():
        y_ref[...] = jnp.zeros(y_ref.shape, F32)


def moe_experts(xn, blk_expert, first, nxt, slot_tok, n_used, w_gate, w_up, w_down, *, layer):
    t, d = xn.shape
    hidden = w_gate.shape[-1]
    n_steps = blk_expert.shape[0]
    n_blocks = n_steps - MOE_GATHER_DEPTH + 2
    grid_spec = pltpu.PrefetchScalarGridSpec(
        num_scalar_prefetch=5,
        grid=(n_steps,),
        in_specs=[pl.BlockSpec(memory_space=pl.ANY)] * 4,
        out_specs=pl.BlockSpec((MOE_ROWS, d), lambda i, *_: (jnp.minimum(i, n_blocks - 1), 0)),
        scratch_shapes=[pltpu.VMEM((d, hidden), F32),
                        pltpu.VMEM((d, hidden), F32),
                        pltpu.VMEM((hidden, d), F32),
                        pltpu.VMEM((d, hidden), BF16),
                        pltpu.VMEM((d, hidden), BF16),
                        pltpu.VMEM((hidden, d), BF16),
                        pltpu.VMEM((MOE_GATHER_DEPTH, MOE_ROWS, d), F32),
                        pltpu.SemaphoreType.DMA((3,)),
                        pltpu.SemaphoreType.DMA((MOE_GATHER_DEPTH,))],
    )
    return pl.pallas_call(
        functools.partial(_expert_kernel, layer=layer),
        grid_spec=grid_spec,
        out_shape=jax.ShapeDtypeStruct((n_blocks * MOE_ROWS, d), F32),
        compiler_params=_params("arbitrary"),
        name="moe_experts",
    )(blk_expert, first, nxt, slot_tok, n_used, xn, w_gate, w_up, w_down)


def _combine_kernel(dest0_ref, dest1_ref, h_ref, gate_ref, y_hbm, *rest, tm, emit_norm):
    o_ref = rest[0]
    buf_ref, sem_ref = rest[-2], rest[-1]
    i = pl.program_id(0)
    n = pl.num_programs(0)
    slot = i % 2

    def gather(step, s):
        _gather_rows(y_hbm, dest0_ref, step * tm, buf_ref.at[s, 0], sem_ref.at[s], tm)
        _gather_rows(y_hbm, dest1_ref, step * tm, buf_ref.at[s, 1], sem_ref.at[s], tm)

    @pl.when(i == 0)
    def _():
        gather(0, 0)

    @pl.when(i + 1 < n)
    def _():
        gather(i + 1, 1 - slot)

    _wait_rows(y_hbm, buf_ref.at[slot, 0], sem_ref.at[slot], tm)
    _wait_rows(y_hbm, buf_ref.at[slot, 1], sem_ref.at[slot], tm)
    gates = gate_ref[...]
    out = h_ref[...] + (gates[:, 0:1] * buf_ref[slot, 0] + gates[:, 1:2] * buf_ref[slot, 1])
    o_ref[...] = out
    if emit_norm:
        rest[1][...] = _rms_rows(out).astype(BF16)


def moe_combine(h, gates, ys, dest0, dest1, *, emit_norm, tm=256):
    t, d = h.shape
    out_specs = [pl.BlockSpec((tm, d), lambda i, d0, d1: (i, 0))]
    out_shape = [jax.ShapeDtypeStruct((t, d), F32)]
    if emit_norm:
        out_specs.append(pl.BlockSpec((tm, d), lambda i, d0, d1: (i, 0)))
        out_shape.append(jax.ShapeDtypeStruct((t, d), BF16))
    grid_spec = pltpu.PrefetchScalarGridSpec(
        num_scalar_prefetch=2,
        grid=(t // tm,),
        in_specs=[pl.BlockSpec((tm, d), lambda i, d0, d1: (i, 0)),
                  pl.BlockSpec((tm, LANES), lambda i, d0, d1: (i, 0)),
                  pl.BlockSpec(memory_space=pl.ANY)],
        out_specs=out_specs,
        scratch_shapes=[pltpu.VMEM((2, 2, tm, d), F32),
                        pltpu.SemaphoreType.DMA((2,))],
    )
    return pl.pallas_call(
        functools.partial(_combine_kernel, tm=tm, emit_norm=emit_norm),
        grid_spec=grid_spec,
        out_shape=out_shape,
        compiler_params=_params("arbitrary"),
        name="moe_combine",
    )(dest0, dest1, h, gates, ys)


def hierarchical_moe(h, gain, w_rg, b_rg, w_re, b_re, w_gate, w_up, w_down, *, layer, emit_norm):
    t, d = h.shape
    xn, gates, eids = moe_router(h, gain, w_rg, b_rg, w_re, b_re)
    eid = eids[:, :MOE_TOP_K].reshape(-1)
    n_assign = t * MOE_TOP_K
    experts = jnp.arange(MOE_EXPERTS, dtype=jnp.int32)
    onehot = (eid[:, None] == experts[None, :]).astype(jnp.int32)
    running = jnp.cumsum(onehot, axis=0)
    rank = jnp.sum(running * onehot, axis=1) - 1
    counts = running[-1]
    padded = (counts + MOE_ROWS - 1) // MOE_ROWS * MOE_ROWS
    pad_end = jnp.cumsum(padded)
    pad_start = pad_end - padded
    dest = (pad_start[eid] + rank).astype(jnp.int32)
    assert n_assign % MOE_ROWS == 0
    n_blocks = n_assign // MOE_ROWS + MOE_EXPERTS
    n_steps = n_blocks + MOE_GATHER_DEPTH - 2
    tok = jnp.arange(n_assign, dtype=jnp.int32) // MOE_TOP_K
    slot_tok = jnp.zeros((n_steps * MOE_ROWS,), jnp.int32).at[dest].set(
        tok, unique_indices=True, mode='promise_in_bounds')
    blk_start = jnp.arange(n_steps, dtype=jnp.int32) * MOE_ROWS
    blk_expert = jnp.minimum(jnp.sum((pad_end[None, :] <= blk_start[:, None]).astype(jnp.int32), axis=1),
                             MOE_EXPERTS - 1).astype(jnp.int32)
    n_used = (pad_end[-1:] // MOE_ROWS).astype(jnp.int32)
    first = (blk_start == pad_start[blk_expert]).astype(jnp.int32)
    later_nonempty = (experts[None, :] > experts[:, None]) & (counts[None, :] > 0)
    next_of_expert = jnp.min(jnp.where(later_nonempty, experts[None, :], MOE_EXPERTS), axis=1)
    next_of_expert = jnp.where(next_of_expert == MOE_EXPERTS, -1, next_of_expert).astype(jnp.int32)
    nxt = next_of_expert[blk_expert]
    ys = moe_experts(xn, blk_expert, first, nxt, slot_tok, n_used, w_gate, w_up, w_down, layer=layer)
    dest2 = dest.reshape(t, MOE_TOP_K)
    return moe_combine(h, gates, ys, dest2[:, 0], dest2[:, 1], emit_norm=emit_norm)


def kernel(x, norm_a, w_in_a, conv_a, a_log_a, dt_bias_a, out_norm_a, w_out_a, norm_kv, w_kv, k_norm_b, norm_b, w_q_b, q_norm_b, w_out_b, norm_moe, w_router_group, b_router_group, w_router_expert, b_router_expert, w_gate, w_up, w_down):
    batch, seq, d = x.shape
    depth = norm_moe.shape[0]
    n_a = norm_a.shape[0]
    n_kheads = d // HEAD_DIM
    kdim = n_kheads * HEAD_DIM
    n_main = 6 * kdim
    h = x.reshape(batch * seq, d)
    hn = None
    kv = None
    for layer in range(depth):
        if layer < n_a:
            i = layer
            xn, gates = gdn_gates(h, norm_a[i], w_in_a[i][:, n_main:], a_log_a[i], dt_bias_a[i])
            proj = matmul(xn, w_in_a[i], n_cols=n_main, tm=1024, tn=1024)
            o = gdn_core(proj, gates, conv_a[i], out_norm_a[i], batch=batch, seq=seq, n_kheads=n_kheads)
            h = matmul(o, w_out_a[i], res=h, tm=512, tn=512)
        else:
            i = layer - n_a
            if i == 0:
                kv = matmul(hn, w_kv, scale=norm_kv, tm=1024, tn=1024)
            q = matmul(hn, w_q_b[i], scale=norm_b[i], tm=1024, tn=1024)
            o = sb_attention(q, kv, q_norm_b[i], k_norm_b, batch=batch, seq=seq, n_heads=n_kheads)
            h = matmul(o, w_out_b[i], res=h, tm=1024, tn=1024)
        emit_norm = n_a <= layer + 1 < depth
        out = hierarchical_moe(h, norm_moe[layer], w_router_group[layer], b_router_group[layer],
                               w_router_expert[layer], b_router_expert[layer],
                               w_gate, w_up, w_down, layer=layer, emit_norm=emit_norm)
        h, hn = (out[0], out[1]) if emit_norm else (out[0], None)
    return h.reshape(batch, seq, d)
```

```python
import functools

import jax
import jax.numpy as jnp
from jax import lax
from jax.experimental import pallas as pl
from jax.experimental.pallas import tpu as pltpu

NORM_EPS = 1e-6
LOG2E = 1.4426950408889634
HEAD_DIM = 128
GDN_CONV = 4
GDN_CHUNK = 64
GDN_GROUP = 16
SB_TILE = 256
MOE_GROUPS = 4
MOE_EXPERTS_PER_GROUP = 8
MOE_EXPERTS = MOE_GROUPS * MOE_EXPERTS_PER_GROUP
MOE_TOP_K = 2
MOE_ROWS = 256
MOE_GATHER_DEPTH = 3
MOE_WEIGHT_SPLIT = 4
LANES = 128
VMEM_LIMIT = 56 * 1024 * 1024

BF16 = jnp.bfloat16
F32 = jnp.float32


def _params(*semantics):
    return pltpu.CompilerParams(dimension_semantics=semantics, vmem_limit_bytes=VMEM_LIMIT)


def _silu(x):
    return x * (1.0 / (1.0 + jnp.exp(-x)))


def _softplus(x):
    return jnp.maximum(x, 0.0) + jnp.log(1.0 + jnp.exp(-jnp.abs(x)))


def _rms_rows(x):
    return x * lax.rsqrt(jnp.mean(x * x, axis=-1, keepdims=True) + NORM_EPS)


def _bdot(a, b):
    return jnp.einsum('nik,nkj->nij', a.astype(BF16), b.astype(BF16), preferred_element_type=F32)


def _matmul_kernel(*refs, has_scale, has_res, w_rows_are_outputs):
    a_ref, w_ref = refs[0], refs[1]
    scale_ref = refs[2] if has_scale else None
    res_ref = refs[2 + has_scale] if has_res else None
    o_ref, wb_ref = refs[-2], refs[-1]

    @pl.when(pl.program_id(1) == 0)
    def _():
        w = w_ref[...]
        if has_scale:
            w = w * scale_ref[...]
        wb_ref[...] = w.astype(BF16)

    w_contract = 1 if w_rows_are_outputs else 0
    acc = lax.dot_general(a_ref[...], wb_ref[...], (((1,), (w_contract,)), ((), ())),
                          preferred_element_type=F32)
    if has_res:
        acc = res_ref[...] + acc
    o_ref[...] = acc.astype(o_ref.dtype)


def matmul(a, w, *, n_cols=None, scale=None, res=None, w_transposed=False, tm, tn, out_dtype=F32):
    t, k = a.shape
    n = w.shape[0 if w_transposed else 1] if n_cols is None else n_cols
    tn = min(tn, n)
    while n % tn:
        tn //= 2
    assert t % tm == 0 and tn % LANES == 0, (t, tm, n, tn)
    assert not (w_transposed and scale is not None)
    w_block = (tn, k) if w_transposed else (k, tn)
    w_index = (lambda j, i: (j, 0)) if w_transposed else (lambda j, i: (0, j))
    in_specs = [pl.BlockSpec((tm, k), lambda j, i: (i, 0)),
                pl.BlockSpec(w_block, w_index)]
    args = [a, w]
    if scale is not None:
        in_specs.append(pl.BlockSpec((k, 1), lambda j, i: (0, 0)))
        args.append(scale.reshape(k, 1))
    if res is not None:
        in_specs.append(pl.BlockSpec((tm, tn), lambda j, i: (i, j)))
        args.append(res)
    return pl.pallas_call(
        functools.partial(_matmul_kernel, has_scale=scale is not None, has_res=res is not None,
                          w_rows_are_outputs=w_transposed),
        grid=(n // tn, t // tm),
        in_specs=in_specs,
        out_specs=pl.BlockSpec((tm, tn), lambda j, i: (i, j)),
        out_shape=jax.ShapeDtypeStruct((t, n), out_dtype),
        scratch_shapes=[pltpu.VMEM(w_block, BF16)],
        compiler_params=_params("arbitrary", "arbitrary"),
        name="matmul",
    )(*args)


def _gdn_gates_kernel(x_ref, g_ref, w_hbm, alog_ref, dt_ref, xn_ref, o_ref, ws_ref, wb_ref, sem_ref, *,
                      n_heads, col0):
    @pl.when(pl.program_id(0) == 0)
    def _():
        cp = pltpu.make_async_copy(w_hbm.at[pl.ds(col0, 2 * n_heads), :], ws_ref, sem_ref)
        cp.start()
        cp.wait()
        wb_ref[...] = jnp.zeros(wb_ref.shape, BF16)
        wb_ref[0:2 * n_heads, :] = ws_ref[...].astype(BF16)

    xn = (_rms_rows(x_ref[...]) * g_ref[...]).astype(BF16)
    xn_ref[...] = xn
    y = lax.dot_general(xn, wb_ref[...], (((1,), (1,)), ((), ())),
                        preferred_element_type=F32)
    lane = lax.broadcasted_iota(jnp.int32, y.shape, 1)
    row = lax.broadcasted_iota(jnp.int32, y.shape, 0)
    beta = 1.0 / (1.0 + jnp.exp(-y))
    g = -jnp.exp(alog_ref[...]) * _softplus(y + dt_ref[...])
    pos = row & (GDN_CHUNK - 1)
    shift = 1
    while shift < GDN_CHUNK:
        g = g + jnp.where(pos >= shift, pltpu.roll(g, shift, axis=0), 0.0)
        shift *= 2
    o_ref[...] = jnp.where(lane < n_heads, beta, jnp.where(lane < 2 * n_heads, g, 0.0))


def gdn_gates(x, gain, w_in_t, col0, a_log, dt_bias, *, tm=512):
    t, d = x.shape
    hv = a_log.shape[0]
    row_param = lambda p: jnp.zeros((1, LANES), F32).at[0, hv:2 * hv].set(p.astype(F32))
    return pl.pallas_call(
        functools.partial(_gdn_gates_kernel, n_heads=hv, col0=col0),
        grid=(t // tm,),
        in_specs=[pl.BlockSpec((tm, d), lambda i: (i, 0)),
                  pl.BlockSpec((1, d), lambda i: (0, 0)),
                  pl.BlockSpec(memory_space=pl.ANY),
                  pl.BlockSpec((1, LANES), lambda i: (0, 0)),
                  pl.BlockSpec((1, LANES), lambda i: (0, 0))],
        out_specs=[pl.BlockSpec((tm, d), lambda i: (i, 0)),
                   pl.BlockSpec((tm, LANES), lambda i: (i, 0))],
        out_shape=[jax.ShapeDtypeStruct((t, d), BF16),
                   jax.ShapeDtypeStruct((t, LANES), F32)],
        scratch_shapes=[pltpu.VMEM((2 * hv, d), F32),
                        pltpu.VMEM((LANES, d), BF16),
                        pltpu.SemaphoreType.DMA(())],
        compiler_params=_params("arbitrary"),
        name="gdn_gates",
    )(x, gain.reshape(1, d), w_in_t, row_param(a_log), row_param(dt_bias))


def _unit_lower_inverse(low):
    c = low.shape[-1]
    ri = lax.broadcasted_iota(jnp.int32, low.shape, 1)
    ci = lax.broadcasted_iota(jnp.int32, low.shape, 2)
    eye = jnp.where(ri == ci, 1.0, 0.0).astype(F32)
    x = eye
    s = 1
    while s < c:
        join = ((ri & -(2 * s)) == (ci & -(2 * s))) & ((ri & s) != 0) & ((ci & s) == 0)
        cs = jnp.where(join, low, 0.0)
        if s == 1:
            x = eye - cs
        else:
            xb = x.astype(BF16)
            x = x - _bdot(xb, _bdot(cs, xb))
        s *= 2
    return x


def _gdn_kernel(q_ref, k_ref, v_ref, z_ref, gb_ref, gct_ref, cq_ref, ck_ref, cv_ref, gain_ref, o_ref,
                pad_ref, qs_ref, ks_ref, kt_ref, vs_ref, bcol_ref, gcol_ref,
                pq_ref, n_ref, op_ref, d_ref, *, n_vheads):
    seq = q_ref.shape[0]
    n_chunks = seq // GDN_CHUNK
    c = GDN_CHUNK
    grp_chunks = GDN_GROUP
    h = pl.program_id(1)

    grp_rows = grp_chunks * c
    pad_ref[0:8, :] = jnp.zeros((8, pad_ref.shape[1]), F32)

    def l2n(x):
        return x * lax.rsqrt(jnp.sum(x * x, axis=-1, keepdims=True) + NORM_EPS)

    def prep(grp):
        r0 = grp * grp_rows
        sl = slice(r0, r0 + grp_rows)

        def conv_silu(x_ref, cw_ref, col0, width):
            cols = slice(col0, col0 + width)
            pad_ref[8 + r0:8 + r0 + grp_rows, cols] = x_ref[sl, :]
            y = pad_ref[5 + r0:5 + r0 + grp_rows, cols] * cw_ref[0:1, :]
            for j in range(1, GDN_CONV):
                y = y + pad_ref[5 + j + r0:5 + j + r0 + grp_rows, cols] * cw_ref[j:j + 1, :]
            return _silu(y)

        qs_ref[sl, :] = l2n(conv_silu(q_ref, cq_ref, 0, HEAD_DIM)) * (HEAD_DIM ** -0.5)
        kk_ = l2n(conv_silu(k_ref, ck_ref, HEAD_DIM, HEAD_DIM))
        ks_ref[sl, :] = kk_
        vs_ref[sl, :] = conv_silu(v_ref, cv_ref, 2 * HEAD_DIM, 2 * HEAD_DIM)
        for ci_ in range(grp_chunks):
            kt_ref[grp * grp_chunks + ci_] = kk_[ci_ * c:(ci_ + 1) * c, :].T.astype(BF16)
        gb = gb_ref[sl, :]
        lane = lax.broadcasted_iota(jnp.int32, gb.shape, 1)
        for r in range(2):
            hv = 2 * h + r
            bcol = jnp.sum(jnp.where(lane == hv, gb, 0.0), axis=1, keepdims=True)
            gcol = jnp.sum(jnp.where(lane == n_vheads + hv, gb, 0.0), axis=1, keepdims=True)
            bcol_ref[r, sl, :] = jnp.broadcast_to(bcol, (grp_rows, HEAD_DIM))
            gcol_ref[r, sl, :] = jnp.broadcast_to(gcol, (grp_rows, HEAD_DIM))

    def phase1(grp):
        g = grp_chunks
        rows = slice(grp * grp_rows, (grp + 1) * grp_rows)
        chunks = slice(grp * g, (grp + 1) * g)
        q3 = qs_ref[rows, :].reshape(g, c, HEAD_DIM)
        k3 = ks_ref[rows, :].reshape(g, c, HEAD_DIM)
        kt = kt_ref[chunks]
        qkk = _bdot(jnp.concatenate([q3, k3], axis=1), kt)
        qk, kk = qkk[:, :c], qkk[:, c:]
        ri = lax.broadcasted_iota(jnp.int32, (g, c, c), 1)
        ci = lax.broadcasted_iota(jnp.int32, (g, c, c), 2)
        incl = ri >= ci
        lows, rhss, attns, kdts, qgs = [], [], [], [], []
        for r in range(2):
            b = bcol_ref[r, rows, :].reshape(g, c, HEAD_DIM)
            gcf = gcol_ref[r, rows, :].reshape(g, c, HEAD_DIM)
            grow = gct_ref[0, r, chunks]
            glast = gcf[:, c - 1:c, :]
            diff = gcf[:, :, :c] - grow
            decay = jnp.where(incl, jnp.exp(jnp.where(incl, diff, 0.0)), 0.0)
            lows.append(jnp.where(ri > ci, b[:, :, :c] * kk * decay, 0.0))
            attns.append(qk * decay)
            egc = jnp.exp(gcf)
            v3 = vs_ref[rows, r * HEAD_DIM:(r + 1) * HEAD_DIM].reshape(g, c, HEAD_DIM)
            rhss.append(jnp.concatenate([v3 * b, k3 * b * egc], axis=2))
            kdts.append(kt.astype(F32) * jnp.exp(glast[:, :, :c] - grow))
            qgs.append(q3 * egc)
            d_ref[r, chunks] = jnp.exp(glast)
        tinv = _unit_lower_inverse(jnp.concatenate(lows, axis=0))
        uw = _bdot(tinv, jnp.concatenate(rhss, axis=0)).astype(BF16)
        auw = _bdot(jnp.concatenate(attns, axis=0), uw)
        kuw = _bdot(jnp.concatenate(kdts, axis=0), uw)
        for r in range(2):
            sel = slice(r * g, (r + 1) * g)
            pq_ref[r, chunks] = jnp.concatenate(
                [kuw[sel, :, HEAD_DIM:], qgs[r] - auw[sel, :, HEAD_DIM:]], axis=1).astype(BF16)
            n_ref[r, chunks] = kuw[sel, :, :HEAD_DIM]
            op_ref[r, rows, :] = auw[sel, :, :HEAD_DIM].reshape(g * c, HEAD_DIM)

    for grp in range(n_chunks // grp_chunks):
        prep(grp)
        phase1(grp)

    gain = gain_ref[...]
    state = jnp.zeros((2, HEAD_DIM, HEAD_DIM), F32)
    for ch in range(n_chunks):
        rows = slice(ch * c, (ch + 1) * c)
        ps = _bdot(pq_ref[:, ch], state)
        state = state * d_ref[:, ch] - ps[:, :HEAD_DIM] + n_ref[:, ch]
        o = op_ref[:, rows, :] + ps[:, HEAD_DIM:]
        for r in range(2):
            zc = z_ref[rows, r * HEAD_DIM:(r + 1) * HEAD_DIM]
            o_ref[rows, r * HEAD_DIM:(r + 1) * HEAD_DIM] = (_rms_rows(o[r]) * gain * _silu(zc)).astype(o_ref.dtype)


def gdn_core(proj, gates, conv_w, out_gain, *, batch, seq, n_kheads):
    t = proj.shape[0]
    n_vheads = 2 * n_kheads
    kdim = n_kheads * HEAD_DIM
    vdim = 2 * kdim
    n_chunks = seq // GDN_CHUNK
    c = GDN_CHUNK
    gct = gates[:, n_vheads:2 * n_vheads].reshape(batch, n_chunks, c, n_kheads, 2)
    gct = gct.transpose(0, 3, 4, 1, 2).reshape(batch * n_kheads, 2, n_chunks, 1, c)
    kq = kdim // HEAD_DIM
    return pl.pallas_call(
        functools.partial(_gdn_kernel, n_vheads=n_vheads),
        grid=(batch, n_kheads),
        in_specs=[pl.BlockSpec((seq, HEAD_DIM), lambda b, h: (b, h)),
                  pl.BlockSpec((seq, HEAD_DIM), lambda b, h: (b, kq + h)),
                  pl.BlockSpec((seq, 2 * HEAD_DIM), lambda b, h: (b, kq + h)),
                  pl.BlockSpec((seq, 2 * HEAD_DIM), lambda b, h: (b, 2 * kq + h)),
                  pl.BlockSpec((seq, LANES), lambda b, h: (b, 0)),
                  pl.BlockSpec((1, 2, n_chunks, 1, c), lambda b, h: (b * n_kheads + h, 0, 0, 0, 0)),
                  pl.BlockSpec((GDN_CONV, HEAD_DIM), lambda b, h: (0, h)),
                  pl.BlockSpec((GDN_CONV, HEAD_DIM), lambda b, h: (0, kq + h)),
                  pl.BlockSpec((GDN_CONV, 2 * HEAD_DIM), lambda b, h: (0, kq + h)),
                  pl.BlockSpec((1, HEAD_DIM), lambda b, h: (0, 0))],
        out_specs=pl.BlockSpec((seq, 2 * HEAD_DIM), lambda b, h: (b, h)),
        out_shape=jax.ShapeDtypeStruct((t, vdim), BF16),
        scratch_shapes=[pltpu.VMEM((seq + 8, 4 * HEAD_DIM), F32),
                        pltpu.VMEM((seq, HEAD_DIM), F32),
                        pltpu.VMEM((seq, HEAD_DIM), F32),
                        pltpu.VMEM((n_chunks, HEAD_DIM, c), BF16),
                        pltpu.VMEM((seq, 2 * HEAD_DIM), F32),
                        pltpu.VMEM((2, seq, HEAD_DIM), F32),
                        pltpu.VMEM((2, seq, HEAD_DIM), F32),
                        pltpu.VMEM((2, n_chunks, HEAD_DIM + c, HEAD_DIM), BF16),
                        pltpu.VMEM((2, n_chunks, HEAD_DIM, HEAD_DIM), F32),
                        pltpu.VMEM((2, seq, HEAD_DIM), F32),
                        pltpu.VMEM((2, n_chunks, 1, HEAD_DIM), F32)],
        compiler_params=_params("arbitrary", "arbitrary"),
        name="gdn_core",
    )(proj, proj, proj, proj, gates, gct, conv_w, conv_w, conv_w, out_gain.reshape(1, HEAD_DIM))


def _sb_kernel(q_ref, k_ref, v_ref, qg_ref, kg_ref, o_ref, qs_ref, ks_ref, vs_ref, tri_ref, z_ref, p_ref):
    seq = q_ref.shape[0]
    t = SB_TILE
    qs_ref[...] = (_rms_rows(q_ref[...]) * qg_ref[...] * (-LOG2E * HEAD_DIM ** -0.5)).astype(BF16)
    ks_ref[...] = (_rms_rows(k_ref[...]) * kg_ref[...]).astype(BF16)
    vs_ref[...] = v_ref[...].astype(BF16)
    ri = lax.broadcasted_iota(jnp.int32, (t, t), 0)
    ci = lax.broadcasted_iota(jnp.int32, (t, t), 1)
    tri_ref[...] = jnp.where(ri > ci, 1.0, 0.0).astype(BF16)
    causal = ci < ri

    for qi in range(seq // t):
        width = (qi + 1) * t
        z_ref[:, 0:width] = lax.dot_general(qs_ref[qi * t:(qi + 1) * t, :], ks_ref[0:width, :],
                                            (((1,), (1,)), ((), ())), preferred_element_type=F32)
        later0 = jnp.zeros((t, 1), F32)
        for s in range(qi, -1, -1):
            zn = z_ref[:, s * t:(s + 1) * t]
            lf = jnp.minimum(zn, 0.0) - jnp.log(1.0 + jnp.exp2(-jnp.abs(zn))) * LOG2E
            if s == qi:
                lf = jnp.where(causal, lf, 0.0)
            later = jnp.dot(lf, tri_ref[...].astype(F32), preferred_element_type=F32) + later0
            wgt = jnp.exp2((lf - zn) + later)
            if s == qi:
                wgt = jnp.where(causal, wgt, 0.0)
            p_ref[:, s * t:(s + 1) * t] = wgt.astype(BF16)
            later0 = later0 + jnp.sum(lf, axis=1, keepdims=True)
        o_ref[qi * t:(qi + 1) * t, :] = jnp.dot(p_ref[:, 0:width], vs_ref[0:width, :],
                                                preferred_element_type=F32).astype(o_ref.dtype)


def sb_attention(q_raw, kv_raw, q_gain, k_gain, *, batch, seq, n_heads):
    t = q_raw.shape[0]
    return pl.pallas_call(
        _sb_kernel,
        grid=(batch, n_heads),
        in_specs=[pl.BlockSpec((seq, HEAD_DIM), lambda b, h: (b, h)),
                  pl.BlockSpec((seq, HEAD_DIM), lambda b, h: (b, h)),
                  pl.BlockSpec((seq, HEAD_DIM), lambda b, h: (b, n_heads + h)),
                  pl.BlockSpec((1, HEAD_DIM), lambda b, h: (0, 0)),
                  pl.BlockSpec((1, HEAD_DIM), lambda b, h: (0, 0))],
        out_specs=pl.BlockSpec((seq, HEAD_DIM), lambda b, h: (b, h)),
        out_shape=jax.ShapeDtypeStruct((t, n_heads * HEAD_DIM), BF16),
        scratch_shapes=[pltpu.VMEM((seq, HEAD_DIM), BF16),
                        pltpu.VMEM((seq, HEAD_DIM), BF16),
                        pltpu.VMEM((seq, HEAD_DIM), BF16),
                        pltpu.VMEM((SB_TILE, SB_TILE), BF16),
                        pltpu.VMEM((SB_TILE, seq), F32),
                        pltpu.VMEM((SB_TILE, seq), BF16)],
        compiler_params=_params("arbitrary", "arbitrary"),
        name="sb_attention",
    )(q_raw, kv_raw, kv_raw, q_gain.reshape(1, HEAD_DIM), k_gain.reshape(1, HEAD_DIM))


def _router_kernel(x_ref, g_ref, w_ref, b_ref, xn_ref, gate_ref, eid_ref):
    xn = _rms_rows(x_ref[...]) * g_ref[...]
    xn_ref[...] = xn
    x_hi = xn.astype(BF16)
    x_lo = (xn - x_hi.astype(F32)).astype(BF16)
    logits = jnp.dot(jnp.concatenate([x_hi, x_lo, x_hi], axis=1), w_ref[...],
                     preferred_element_type=F32) + b_ref[...]
    lane = lax.broadcasted_iota(jnp.int32, logits.shape, 1).astype(F32)
    neg = -jnp.inf
    first_lane = lambda hit: jnp.min(jnp.where(hit, lane, float(LANES)), axis=1, keepdims=True)
    lg = jnp.where(lane < MOE_GROUPS, logits, neg)
    mg = jnp.max(lg, axis=1, keepdims=True)
    g_val = 1.0 / jnp.sum(jnp.exp(lg - mg), axis=1, keepdims=True)
    g_idx = first_lane(lg == mg)
    lo = MOE_GROUPS + MOE_EXPERTS_PER_GROUP * g_idx
    le = jnp.where((lane >= lo) & (lane < lo + MOE_EXPERTS_PER_GROUP), logits, neg)
    m1 = jnp.max(le, axis=1, keepdims=True)
    i1 = first_lane(le == m1)
    le2 = jnp.where(lane == i1, neg, le)
    m2 = jnp.max(le2, axis=1, keepdims=True)
    i2 = first_lane(le2 == m2)
    den = jnp.sum(jnp.exp(le - m1), axis=1, keepdims=True)
    e1 = 1.0 / den
    e2 = jnp.exp(m2 - m1) / den
    w1 = g_val * e1 / (e1 + e2)
    w2 = g_val * e2 / (e1 + e2)
    gate_ref[...] = jnp.where(lane == 0, w1, jnp.where(lane == 1, w2, 0.0))
    eid = jnp.where(lane == 0, i1 - MOE_GROUPS, jnp.where(lane == 1, i2 - MOE_GROUPS, 0.0))
    eid_ref[...] = eid.astype(jnp.int32)


def moe_router(h, gain, w_rg, b_rg, w_re, b_re, *, tm=512):
    t, d = h.shape
    n_logits = MOE_GROUPS + MOE_EXPERTS
    w = jnp.zeros((d, LANES), F32).at[:, :MOE_GROUPS].set(w_rg).at[:, MOE_GROUPS:n_logits].set(w_re)
    w_hi = w.astype(BF16)
    w_lo = (w - w_hi.astype(F32)).astype(BF16)
    w3 = jnp.concatenate([w_hi, w_hi, w_lo], axis=0)
    bias = jnp.zeros((1, LANES), F32).at[0, :MOE_GROUPS].set(b_rg).at[0, MOE_GROUPS:n_logits].set(b_re)
    return pl.pallas_call(
        _router_kernel,
        grid=(t // tm,),
        in_specs=[pl.BlockSpec((tm, d), lambda i: (i, 0)),
                  pl.BlockSpec((1, d), lambda i: (0, 0)),
                  pl.BlockSpec((3 * d, LANES), lambda i: (0, 0)),
                  pl.BlockSpec((1, LANES), lambda i: (0, 0))],
        out_specs=[pl.BlockSpec((tm, d), lambda i: (i, 0)),
                   pl.BlockSpec((tm, LANES), lambda i: (i, 0)),
                   pl.BlockSpec((tm, LANES), lambda i: (i, 0))],
        out_shape=[jax.ShapeDtypeStruct((t, d), F32),
                   jax.ShapeDtypeStruct((t, LANES), F32),
                   jax.ShapeDtypeStruct((t, LANES), jnp.int32)],
        compiler_params=_params("arbitrary"),
        name="moe_router",
    )(h, gain.reshape(1, d), w3, bias)


def _gather_rows(src_hbm, idx_ref, base, dst_ref, sem, n_rows, unrolled=False):
    def issue(r, carry):
        pltpu.make_async_copy(src_hbm.at[pl.ds(idx_ref[base + r], 1), :],
                              dst_ref.at[pl.ds(r, 1), :], sem).start()
        return carry
    if unrolled:
        for r in range(n_rows):
            issue(r, 0)
    else:
        lax.fori_loop(0, n_rows, issue, 0, unroll=8)


def _wait_rows(src_hbm, dst_ref, sem, n_rows):
    pltpu.make_async_copy(src_hbm.at[pl.ds(0, n_rows), :], dst_ref, sem).wait()


def _expert_kernel(blk_expert_ref, first_ref, next_ref, slot_tok_ref, n_used_ref,
                   x_hbm, wg_hbm, wu_hbm, wd_hbm, y_ref,
                   sg_ref, su_ref, sd_ref, wg_ref, wu_ref, wd_ref, xbuf_ref, wsem_ref, sem_ref, *, layer):
    i = pl.program_id(0)
    n_used = n_used_ref[0]
    depth = xbuf_ref.shape[0]
    slot = i % depth

    def weight_copies(e):
        copies = []
        for m, (w_hbm, stage_ref) in enumerate(((wg_hbm, sg_ref), (wu_hbm, su_ref), (wd_hbm, sd_ref))):
            slab = stage_ref.shape[0] // MOE_WEIGHT_SPLIT
            for q in range(MOE_WEIGHT_SPLIT):
                rows = pl.ds(q * slab, slab)
                copies.append(pltpu.make_async_copy(w_hbm.at[layer, e, rows, :], stage_ref.at[rows, :],
                                                    wsem_ref.at[m]))
        return copies

    def gather(blk, s, unrolled):
        _gather_rows(x_hbm, slot_tok_ref, blk * MOE_ROWS, xbuf_ref.at[s], sem_ref.at[s], MOE_ROWS, unrolled)

    def wait_gather(s):
        _wait_rows(x_hbm, xbuf_ref.at[s], sem_ref.at[s], MOE_ROWS)

    @pl.when((i == 0) & (n_used > 0))
    def _():
        for cp in weight_copies(blk_expert_ref[0]):
            cp.start()
        for blk in range(depth - 1):
            gather(blk, blk, False)

    is_first = (i < n_used) & (first_ref[i] == 1)

    @pl.when(is_first)
    def _():
        for cp in weight_copies(blk_expert_ref[i]):
            cp.wait()
        wg_ref[...] = sg_ref[...].astype(BF16)
        wu_ref[...] = su_ref[...].astype(BF16)
        wd_ref[...] = sd_ref[...].astype(BF16)

    @pl.when(is_first & (next_ref[i] >= 0))
    def _():
        for cp in weight_copies(next_ref[i]):
            cp.start(priority=1)

    @pl.when(i < n_used)
    def _():
        wait_gather(slot)
        xb = xbuf_ref[slot].astype(BF16)
        gather(i + depth - 1, (i + depth - 1) % depth, True)
        gate = jnp.dot(xb, wg_ref[...], preferred_element_type=F32)
        up = jnp.dot(xb, wu_ref[...], preferred_element_type=F32)
        hb = (_silu(gate) * up).astype(BF16)
        y_ref[...] = jnp.dot(hb, wd_ref[...], preferred_element_type=F32)

    @pl.when((i >= n_used) & (i < n_used + depth - 1) & (n_used > 0))
    def _():
        wait_gather(slot)

    @pl.when(i >= n_used)
    def _():
        y_ref[...] = jnp.zeros(y_ref.shape, F32)


def moe_experts(xn, blk_expert, first, nxt, slot_tok, n_used, w_gate, w_up, w_down, *, layer):
    t, d = xn.shape
    hidden = w_gate.shape[-1]
    n_steps = blk_expert.shape[0]
    n_blocks = n_steps - MOE_GATHER_DEPTH + 2
    grid_spec = pltpu.PrefetchScalarGridSpec(
        num_scalar_prefetch=5,
        grid=(n_steps,),
        in_specs=[pl.BlockSpec(memory_space=pl.ANY)] * 4,
        out_specs=pl.BlockSpec((MOE_ROWS, d), lambda i, *_: (jnp.minimum(i, n_blocks - 1), 0)),
        scratch_shapes=[pltpu.VMEM((d, hidden), F32),
                        pltpu.VMEM((d, hidden), F32),
                        pltpu.VMEM((hidden, d), F32),
                        pltpu.VMEM((d, hidden), BF16),
                        pltpu.VMEM((d, hidden), BF16),
                        pltpu.VMEM((hidden, d), BF16),
                        pltpu.VMEM((MOE_GATHER_DEPTH, MOE_ROWS, d), F32),
                        pltpu.SemaphoreType.DMA((3,)),
                        pltpu.SemaphoreType.DMA((MOE_GATHER_DEPTH,))],
    )
    return pl.pallas_call(
        functools.partial(_expert_kernel, layer=layer),
        grid_spec=grid_spec,
        out_shape=jax.ShapeDtypeStruct((n_blocks * MOE_ROWS, d), F32),
        compiler_params=_params("arbitrary"),
        name="moe_experts",
    )(blk_expert, first, nxt, slot_tok, n_used, xn, w_gate, w_up, w_down)


def _combine_kernel(dest0_ref, dest1_ref, h_ref, gate_ref, y_hbm, *rest, tm, emit_norm):
    o_ref = rest[0]
    buf_ref, sem_ref = rest[-2], rest[-1]
    i = pl.program_id(0)
    n = pl.num_programs(0)
    slot = i % 2

    def gather(step, s):
        _gather_rows(y_hbm, dest0_ref, step * tm, buf_ref.at[s, 0], sem_ref.at[s], tm)
        _gather_rows(y_hbm, dest1_ref, step * tm, buf_ref.at[s, 1], sem_ref.at[s], tm)

    @pl.when(i == 0)
    def _():
        gather(0, 0)

    @pl.when(i + 1 < n)
    def _():
        gather(i + 1, 1 - slot)

    _wait_rows(y_hbm, buf_ref.at[slot, 0], sem_ref.at[slot], tm)
    _wait_rows(y_hbm, buf_ref.at[slot, 1], sem_ref.at[slot], tm)
    gates = gate_ref[...]
    out = h_ref[...] + (gates[:, 0:1] * buf_ref[slot, 0] + gates[:, 1:2] * buf_ref[slot, 1])
    o_ref[...] = out
    if emit_norm:
        rest[1][...] = _rms_rows(out).astype(BF16)


def moe_combine(h, gates, ys, dest0, dest1, *, emit_norm, tm=256):
    t, d = h.shape
    out_specs = [pl.BlockSpec((tm, d), lambda i, d0, d1: (i, 0))]
    out_shape = [jax.ShapeDtypeStruct((t, d), F32)]
    if emit_norm:
        out_specs.append(pl.BlockSpec((tm, d), lambda i, d0, d1: (i, 0)))
        out_shape.append(jax.ShapeDtypeStruct((t, d), BF16))
    grid_spec = pltpu.PrefetchScalarGridSpec(
        num_scalar_prefetch=2,
        grid=(t // tm,),
        in_specs=[pl.BlockSpec((tm, d), lambda i, d0, d1: (i, 0)),
                  pl.BlockSpec((tm, LANES), lambda i, d0, d1: (i, 0)),
                  pl.BlockSpec(memory_space=pl.ANY)],
        out_specs=out_specs,
        scratch_shapes=[pltpu.VMEM((2, 2, tm, d), F32),
                        pltpu.SemaphoreType.DMA((2,))],
    )
    return pl.pallas_call(
        functools.partial(_combine_kernel, tm=tm, emit_norm=emit_norm),
        grid_spec=grid_spec,
        out_shape=out_shape,
        compiler_params=_params("arbitrary"),
        name="moe_combine",
    )(dest0, dest1, h, gates, ys)


def hierarchical_moe(h, gain, w_rg, b_rg, w_re, b_re, w_gate, w_up, w_down, *, layer, emit_norm):
    t, d = h.shape
    xn, gates, eids = moe_router(h, gain, w_rg, b_rg, w_re, b_re)
    eid = eids[:, :MOE_TOP_K].reshape(-1)
    n_assign = t * MOE_TOP_K
    experts = jnp.arange(MOE_EXPERTS, dtype=jnp.int32)
    onehot = (eid[:, None] == experts[None, :]).astype(jnp.int32)
    running = jnp.cumsum(onehot, axis=0)
    rank = jnp.sum(running * onehot, axis=1) - 1
    counts = running[-1]
    padded = (counts + MOE_ROWS - 1) // MOE_ROWS * MOE_ROWS
    pad_end = jnp.cumsum(padded)
    pad_start = pad_end - padded
    dest = (pad_start[eid] + rank).astype(jnp.int32)
    assert n_assign % MOE_ROWS == 0
    n_blocks = n_assign // MOE_ROWS + MOE_EXPERTS
    n_steps = n_blocks + MOE_GATHER_DEPTH - 2
    tok = jnp.arange(n_assign, dtype=jnp.int32) // MOE_TOP_K
    slot_tok = jnp.zeros((n_steps * MOE_ROWS,), jnp.int32).at[dest].set(
        tok, unique_indices=True, mode='promise_in_bounds')
    blk_start = jnp.arange(n_steps, dtype=jnp.int32) * MOE_ROWS
    blk_expert = jnp.minimum(jnp.sum((pad_end[None, :] <= blk_start[:, None]).astype(jnp.int32), axis=1),
                             MOE_EXPERTS - 1).astype(jnp.int32)
    n_used = (pad_end[-1:] // MOE_ROWS).astype(jnp.int32)
    first = (blk_start == pad_start[blk_expert]).astype(jnp.int32)
    later_nonempty = (experts[None, :] > experts[:, None]) & (counts[None, :] > 0)
    next_of_expert = jnp.min(jnp.where(later_nonempty, experts[None, :], MOE_EXPERTS), axis=1)
    next_of_expert = jnp.where(next_of_expert == MOE_EXPERTS, -1, next_of_expert).astype(jnp.int32)
    nxt = next_of_expert[blk_expert]
    ys = moe_experts(xn, blk_expert, first, nxt, slot_tok, n_used, w_gate, w_up, w_down, layer=layer)
    dest2 = dest.reshape(t, MOE_TOP_K)
    return moe_combine(h, gates, ys, dest2[:, 0], dest2[:, 1], emit_norm=emit_norm)


def kernel(x, norm_a, w_in_a, conv_a, a_log_a, dt_bias_a, out_norm_a, w_out_a, norm_kv, w_kv, k_norm_b, norm_b, w_q_b, q_norm_b, w_out_b, norm_moe, w_router_group, b_router_group, w_router_expert, b_router_expert, w_gate, w_up, w_down):
    batch, seq, d = x.shape
    depth = norm_moe.shape[0]
    n_a = norm_a.shape[0]
    n_kheads = d // HEAD_DIM
    kdim = n_kheads * HEAD_DIM
    n_main = 6 * kdim
    h = x.reshape(batch * seq, d)
    hn = None
    kv = None
    for layer in range(depth):
        if layer < n_a:
            i = layer
            w_in_t = jnp.swapaxes(w_in_a[i], 0, 1)
            xn, gates = gdn_gates(h, norm_a[i], w_in_t, n_main, a_log_a[i], dt_bias_a[i])
            proj = matmul(xn, w_in_t, n_cols=n_main, w_transposed=True, tm=1024, tn=1024)
            o = gdn_core(proj, gates, conv_a[i], out_norm_a[i], batch=batch, seq=seq, n_kheads=n_kheads)
            h = matmul(o, w_out_a[i], res=h, tm=512, tn=512)
        else:
            i = layer - n_a
            if i == 0:
                kv = matmul(hn, w_kv, scale=norm_kv, tm=1024, tn=1024)
            q = matmul(hn, w_q_b[i], scale=norm_b[i], tm=1024, tn=1024)
            o = sb_attention(q, kv, q_norm_b[i], k_norm_b, batch=batch, seq=seq, n_heads=n_kheads)
            h = matmul(o, w_out_b[i], res=h, tm=1024, tn=1024)
        emit_norm = n_a <= layer + 1 < depth
        out = hierarchical_moe(h, norm_moe[layer], w_router_group[layer], b_router_group[layer],
                               w_router_expert[layer], b_router_expert[layer],
                               w_gate, w_up, w_down, layer=layer, emit_norm=emit_norm)
        h, hn = (out[0], out[1]) if emit_norm else (out[0], None)
    return h.reshape(batch, seq, d)
```

```python
import functools

import jax
import jax.numpy as jnp
from jax import lax
from jax.experimental import pallas as pl
from jax.experimental.pallas import tpu as pltpu

NORM_EPS = 1e-6
LOG2E = 1.4426950408889634
HEAD_DIM = 128
GDN_CONV = 4
GDN_CHUNK = 64
GDN_GROUP = 16
SB_TILE = 256
MOE_GROUPS = 4
MOE_EXPERTS_PER_GROUP = 8
MOE_EXPERTS = MOE_GROUPS * MOE_EXPERTS_PER_GROUP
MOE_TOP_K = 2
MOE_ROWS = 256
MOE_GATHER_DEPTH = 3
MOE_WEIGHT_SPLIT = 4
LANES = 128
VMEM_LIMIT = 56 * 1024 * 1024

BF16 = jnp.bfloat16
F32 = jnp.float32


def _params(*semantics):
    return pltpu.CompilerParams(dimension_semantics=semantics, vmem_limit_bytes=VMEM_LIMIT)


def _silu(x):
    return x * (1.0 / (1.0 + jnp.exp(-x)))


def _softplus(x):
    return jnp.maximum(x, 0.0) + jnp.log(1.0 + jnp.exp(-jnp.abs(x)))


def _rms_rows(x):
    return x * lax.rsqrt(jnp.mean(x * x, axis=-1, keepdims=True) + NORM_EPS)


def _bdot(a, b):
    return jnp.einsum('nik,nkj->nij', a.astype(BF16), b.astype(BF16), preferred_element_type=F32)


def _matmul_kernel(*refs, has_scale, has_res, w_rows_are_outputs):
    a_ref, w_ref = refs[0], refs[1]
    scale_ref = refs[2] if has_scale else None
    res_ref = refs[2 + has_scale] if has_res else None
    o_ref, wb_ref = refs[-2], refs[-1]

    @pl.when(pl.program_id(1) == 0)
    def _():
        w = w_ref[...]
        if has_scale:
            w = w * scale_ref[...]
        wb_ref[...] = w.astype(BF16)

    w_contract = 1 if w_rows_are_outputs else 0
    acc = lax.dot_general(a_ref[...], wb_ref[...], (((1,), (w_contract,)), ((), ())),
                          preferred_element_type=F32)
    if has_res:
        acc = res_ref[...] + acc
    o_ref[...] = acc.astype(o_ref.dtype)


def matmul(a, w, *, n_cols=None, scale=None, res=None, w_transposed=False, tm, tn, out_dtype=F32):
    t, k = a.shape
    n = w.shape[0 if w_transposed else 1] if n_cols is None else n_cols
    tn = min(tn, n)
    while n % tn:
        tn //= 2
    assert t % tm == 0 and tn % LANES == 0, (t, tm, n, tn)
    assert not (w_transposed and scale is not None)
    w_block = (tn, k) if w_transposed else (k, tn)
    w_index = (lambda j, i: (j, 0)) if w_transposed else (lambda j, i: (0, j))
    in_specs = [pl.BlockSpec((tm, k), lambda j, i: (i, 0)),
                pl.BlockSpec(w_block, w_index)]
    args = [a, w]
    if scale is not None:
        in_specs.append(pl.BlockSpec((k, 1), lambda j, i: (0, 0)))
        args.append(scale.reshape(k, 1))
    if res is not None:
        in_specs.append(pl.BlockSpec((tm, tn), lambda j, i: (i, j)))
        args.append(res)
    return pl.pallas_call(
        functools.partial(_matmul_kernel, has_scale=scale is not None, has_res=res is not None,
                          w_rows_are_outputs=w_transposed),
        grid=(n // tn, t // tm),
        in_specs=in_specs,
        out_specs=pl.BlockSpec((tm, tn), lambda j, i: (i, j)),
        out_shape=jax.ShapeDtypeStruct((t, n), out_dtype),
        scratch_shapes=[pltpu.VMEM(w_block, BF16)],
        compiler_params=_params("arbitrary", "arbitrary"),
        name="matmul",
    )(*args)


def _gdn_gates_kernel(x_ref, g_ref, w_hbm, alog_ref, dt_ref, xn_ref, o_ref, ws_ref, wb_ref, sem_ref, *,
                      n_heads, col0):
    @pl.when(pl.program_id(0) == 0)
    def _():
        cp = pltpu.make_async_copy(w_hbm.at[pl.ds(col0, 2 * n_heads), :], ws_ref, sem_ref)
        cp.start()
        cp.wait()
        wb_ref[...] = jnp.zeros(wb_ref.shape, BF16)
        wb_ref[0:2 * n_heads, :] = ws_ref[...].astype(BF16)

    xn = (_rms_rows(x_ref[...]) * g_ref[...]).astype(BF16)
    xn_ref[...] = xn
    y = lax.dot_general(xn, wb_ref[...], (((1,), (1,)), ((), ())),
                        preferred_element_type=F32)
    lane = lax.broadcasted_iota(jnp.int32, y.shape, 1)
    row = lax.broadcasted_iota(jnp.int32, y.shape, 0)
    beta = 1.0 / (1.0 + jnp.exp(-y))
    g = -jnp.exp(alog_ref[...]) * _softplus(y + dt_ref[...])
    pos = row & (GDN_CHUNK - 1)
    shift = 1
    while shift < GDN_CHUNK:
        g = g + jnp.where(pos >= shift, pltpu.roll(g, shift, axis=0), 0.0)
        shift *= 2
    o_ref[...] = jnp.where(lane < n_heads, beta, jnp.where(lane < 2 * n_heads, g, 0.0))


def gdn_gates(x, gain, w_in_t, col0, a_log, dt_bias, *, tm=512):
    t, d = x.shape
    hv = a_log.shape[0]
    row_param = lambda p: jnp.zeros((1, LANES), F32).at[0, hv:2 * hv].set(p.astype(F32))
    return pl.pallas_call(
        functools.partial(_gdn_gates_kernel, n_heads=hv, col0=col0),
        grid=(t // tm,),
        in_specs=[pl.BlockSpec((tm, d), lambda i: (i, 0)),
                  pl.BlockSpec((1, d), lambda i: (0, 0)),
                  pl.BlockSpec(memory_space=pl.ANY),
                  pl.BlockSpec((1, LANES), lambda i: (0, 0)),
                  pl.BlockSpec((1, LANES), lambda i: (0, 0))],
        out_specs=[pl.BlockSpec((tm, d), lambda i: (i, 0)),
                   pl.BlockSpec((tm, LANES), lambda i: (i, 0))],
        out_shape=[jax.ShapeDtypeStruct((t, d), BF16),
                   jax.ShapeDtypeStruct((t, LANES), F32)],
        scratch_shapes=[pltpu.VMEM((2 * hv, d), F32),
                        pltpu.VMEM((LANES, d), BF16),
                        pltpu.SemaphoreType.DMA(())],
        compiler_params=_params("arbitrary"),
        name="gdn_gates",
    )(x, gain.reshape(1, d), w_in_t, row_param(a_log), row_param(dt_bias))


def _unit_lower_inverse(low):
    c = low.shape[-1]
    ri = lax.broadcasted_iota(jnp.int32, low.shape, 1)
    ci = lax.broadcasted_iota(jnp.int32, low.shape, 2)
    eye = jnp.where(ri == ci, 1.0, 0.0).astype(F32)
    x = eye
    s = 1
    while s < c:
        join = ((ri & -(2 * s)) == (ci & -(2 * s))) & ((ri & s) != 0) & ((ci & s) == 0)
        cs = jnp.where(join, low, 0.0)
        if s == 1:
            x = eye - cs
        else:
            xb = x.astype(BF16)
            x = x - _bdot(xb, _bdot(cs, xb))
        s *= 2
    return x


def _gdn_kernel(q_ref, k_ref, v_ref, z_ref, gb_ref, gct_ref, cq_ref, ck_ref, cv_ref, gain_ref, o_ref,
                pad_ref, qs_ref, ks_ref, kt_ref, vs_ref, bcol_ref, gcol_ref,
                pq_ref, n_ref, op_ref, d_ref, *, n_vheads):
    seq = q_ref.shape[0]
    n_chunks = seq // GDN_CHUNK
    c = GDN_CHUNK
    grp_chunks = GDN_GROUP
    h = pl.program_id(1)

    grp_rows = grp_chunks * c
    pad_ref[0:8, :] = jnp.zeros((8, pad_ref.shape[1]), F32)

    def l2n(x):
        return x * lax.rsqrt(jnp.sum(x * x, axis=-1, keepdims=True) + NORM_EPS)

    def prep(grp):
        r0 = grp * grp_rows
        sl = slice(r0, r0 + grp_rows)

        def conv_silu(x_ref, cw_ref, col0, width):
            cols = slice(col0, col0 + width)
            pad_ref[8 + r0:8 + r0 + grp_rows, cols] = x_ref[sl, :]
            y = pad_ref[5 + r0:5 + r0 + grp_rows, cols] * cw_ref[0:1, :]
            for j in range(1, GDN_CONV):
                y = y + pad_ref[5 + j + r0:5 + j + r0 + grp_rows, cols] * cw_ref[j:j + 1, :]
            return _silu(y)

        qs_ref[sl, :] = l2n(conv_silu(q_ref, cq_ref, 0, HEAD_DIM)) * (HEAD_DIM ** -0.5)
        kk_ = l2n(conv_silu(k_ref, ck_ref, HEAD_DIM, HEAD_DIM))
        ks_ref[sl, :] = kk_
        vs_ref[sl, :] = conv_silu(v_ref, cv_ref, 2 * HEAD_DIM, 2 * HEAD_DIM)
        for ci_ in range(grp_chunks):
            kt_ref[grp * grp_chunks + ci_] = kk_[ci_ * c:(ci_ + 1) * c, :].T.astype(BF16)
        gb = gb_ref[sl, :]
        lane = lax.broadcasted_iota(jnp.int32, gb.shape, 1)
        for r in range(2):
            hv = 2 * h + r
            bcol = jnp.sum(jnp.where(lane == hv, gb, 0.0), axis=1, keepdims=True)
            gcol = jnp.sum(jnp.where(lane == n_vheads + hv, gb, 0.0), axis=1, keepdims=True)
            bcol_ref[r, sl, :] = jnp.broadcast_to(bcol, (grp_rows, HEAD_DIM))
            gcol_ref[r, sl, :] = jnp.broadcast_to(gcol, (grp_rows, HEAD_DIM))

    def phase1(grp):
        g = grp_chunks
        rows = slice(grp * grp_rows, (grp + 1) * grp_rows)
        chunks = slice(grp * g, (grp + 1) * g)
        q3 = qs_ref[rows, :].reshape(g, c, HEAD_DIM)
        k3 = ks_ref[rows, :].reshape(g, c, HEAD_DIM)
        kt = kt_ref[chunks]
        qkk = _bdot(jnp.concatenate([q3, k3], axis=1), kt)
        qk, kk = qkk[:, :c], qkk[:, c:]
        ri = lax.broadcasted_iota(jnp.int32, (g, c, c), 1)
        ci = lax.broadcasted_iota(jnp.int32, (g, c, c), 2)
        incl = ri >= ci
        lows, rhss, attns, kdts, qgs = [], [], [], [], []
        for r in range(2):
            b = bcol_ref[r, rows, :].reshape(g, c, HEAD_DIM)
            gcf = gcol_ref[r, rows, :].reshape(g, c, HEAD_DIM)
            grow = gct_ref[0, r, chunks]
            glast = gcf[:, c - 1:c, :]
            diff = gcf[:, :, :c] - grow
            decay = jnp.where(incl, jnp.exp(jnp.where(incl, diff, 0.0)), 0.0)
            lows.append(jnp.where(ri > ci, b[:, :, :c] * kk * decay, 0.0))
            attns.append(qk * decay)
            egc = jnp.exp(gcf)
            v3 = vs_ref[rows, r * HEAD_DIM:(r + 1) * HEAD_DIM].reshape(g, c, HEAD_DIM)
            rhss.append(jnp.concatenate([v3 * b, k3 * b * egc], axis=2))
            kdts.append(kt.astype(F32) * jnp.exp(glast[:, :, :c] - grow))
            qgs.append(q3 * egc)
            d_ref[r, chunks] = jnp.exp(glast)
        tinv = _unit_lower_inverse(jnp.concatenate(lows, axis=0))
        uw = _bdot(tinv, jnp.concatenate(rhss, axis=0)).astype(BF16)
        auw = _bdot(jnp.concatenate(attns, axis=0), uw)
        kuw = _bdot(jnp.concatenate(kdts, axis=0), uw)
        for r in range(2):
            sel = slice(r * g, (r + 1) * g)
            pq_ref[r, chunks] = jnp.concatenate(
                [kuw[sel, :, HEAD_DIM:], qgs[r] - auw[sel, :, HEAD_DIM:]], axis=1).astype(BF16)
            n_ref[r, chunks] = kuw[sel, :, :HEAD_DIM]
            op_ref[r, rows, :] = auw[sel, :, :HEAD_DIM].reshape(g * c, HEAD_DIM)

    for grp in range(n_chunks // grp_chunks):
        prep(grp)
        phase1(grp)

    gain = gain_ref[...]
    state = jnp.zeros((2, HEAD_DIM, HEAD_DIM), F32)
    for ch in range(n_chunks):
        rows = slice(ch * c, (ch + 1) * c)
        ps = _bdot(pq_ref[:, ch], state)
        state = state * d_ref[:, ch] - ps[:, :HEAD_DIM] + n_ref[:, ch]
        o = op_ref[:, rows, :] + ps[:, HEAD_DIM:]
        for r in range(2):
            zc = z_ref[rows, r * HEAD_DIM:(r + 1) * HEAD_DIM]
            o_ref[rows, r * HEAD_DIM:(r + 1) * HEAD_DIM] = (_rms_rows(o[r]) * gain * _silu(zc)).astype(o_ref.dtype)


def gdn_core(proj, gates, conv_w, out_gain, *, batch, seq, n_kheads):
    t = proj.shape[0]
    n_vheads = 2 * n_kheads
    kdim = n_kheads * HEAD_DIM
    vdim = 2 * kdim
    n_chunks = seq // GDN_CHUNK
    c = GDN_CHUNK
    gct = gates[:, n_vheads:2 * n_vheads].reshape(batch, n_chunks, c, n_kheads, 2)
    gct = gct.transpose(0, 3, 4, 1, 2).reshape(batch * n_kheads, 2, n_chunks, 1, c)
    kq = kdim // HEAD_DIM
    return pl.pallas_call(
        functools.partial(_gdn_kernel, n_vheads=n_vheads),
        grid=(batch, n_kheads),
        in_specs=[pl.BlockSpec((seq, HEAD_DIM), lambda b, h: (b, h)),
                  pl.BlockSpec((seq, HEAD_DIM), lambda b, h: (b, kq + h)),
                  pl.BlockSpec((seq, 2 * HEAD_DIM), lambda b, h: (b, kq + h)),
                  pl.BlockSpec((seq, 2 * HEAD_DIM), lambda b, h: (b, 2 * kq + h)),
                  pl.BlockSpec((seq, LANES), lambda b, h: (b, 0)),
                  pl.BlockSpec((1, 2, n_chunks, 1, c), lambda b, h: (b * n_kheads + h, 0, 0, 0, 0)),
                  pl.BlockSpec((GDN_CONV, HEAD_DIM), lambda b, h: (0, h)),
                  pl.BlockSpec((GDN_CONV, HEAD_DIM), lambda b, h: (0, kq + h)),
                  pl.BlockSpec((GDN_CONV, 2 * HEAD_DIM), lambda b, h: (0, kq + h)),
                  pl.BlockSpec((1, HEAD_DIM), lambda b, h: (0, 0))],
        out_specs=pl.BlockSpec((seq, 2 * HEAD_DIM), lambda b, h: (b, h)),
        out_shape=jax.ShapeDtypeStruct((t, vdim), BF16),
        scratch_shapes=[pltpu.VMEM((seq + 8, 4 * HEAD_DIM), F32),
                        pltpu.VMEM((seq, HEAD_DIM), F32),
                        pltpu.VMEM((seq, HEAD_DIM), F32),
                        pltpu.VMEM((n_chunks, HEAD_DIM, c), BF16),
                        pltpu.VMEM((seq, 2 * HEAD_DIM), F32),
                        pltpu.VMEM((2, seq, HEAD_DIM), F32),
                        pltpu.VMEM((2, seq, HEAD_DIM), F32),
                        pltpu.VMEM((2, n_chunks, HEAD_DIM + c, HEAD_DIM), BF16),
                        pltpu.VMEM((2, n_chunks, HEAD_DIM, HEAD_DIM), F32),
                        pltpu.VMEM((2, seq, HEAD_DIM), F32),
                        pltpu.VMEM((2, n_chunks, 1, HEAD_DIM), F32)],
        compiler_params=_params("arbitrary", "arbitrary"),
        name="gdn_core",
    )(proj, proj, proj, proj, gates, gct, conv_w, conv_w, conv_w, out_gain.reshape(1, HEAD_DIM))


def _sb_kernel(q_ref, k_ref, v_ref, qg_ref, kg_ref, o_ref, qs_ref, ks_ref, vs_ref, tri_ref, z_ref, p_ref):
    seq = q_ref.shape[0]
    t = SB_TILE
    qs_ref[...] = (_rms_rows(q_ref[...]) * qg_ref[...] * (-LOG2E * HEAD_DIM ** -0.5)).astype(BF16)
    ks_ref[...] = (_rms_rows(k_ref[...]) * kg_ref[...]).astype(BF16)
    vs_ref[...] = v_ref[...].astype(BF16)
    ri = lax.broadcasted_iota(jnp.int32, (t, t), 0)
    ci = lax.broadcasted_iota(jnp.int32, (t, t), 1)
    tri_ref[...] = jnp.where(ri > ci, 1.0, 0.0).astype(BF16)
    causal = ci < ri

    for qi in range(seq // t):
        width = (qi + 1) * t
        z_ref[:, 0:width] = lax.dot_general(qs_ref[qi * t:(qi + 1) * t, :], ks_ref[0:width, :],
                                            (((1,), (1,)), ((), ())), preferred_element_type=F32)
        later0 = jnp.zeros((t, 1), F32)
        for s in range(qi, -1, -1):
            zn = z_ref[:, s * t:(s + 1) * t]
            lf = jnp.minimum(zn, 0.0) - jnp.log(1.0 + jnp.exp2(-jnp.abs(zn))) * LOG2E
            if s == qi:
                lf = jnp.where(causal, lf, 0.0)
            later = jnp.dot(lf, tri_ref[...].astype(F32), preferred_element_type=F32) + later0
            wgt = jnp.exp2((lf - zn) + later)
            if s == qi:
                wgt = jnp.where(causal, wgt, 0.0)
            p_ref[:, s * t:(s + 1) * t] = wgt.astype(BF16)
            later0 = later0 + jnp.sum(lf, axis=1, keepdims=True)
        o_ref[qi * t:(qi + 1) * t, :] = jnp.dot(p_ref[:, 0:width], vs_ref[0:width, :],
                                                preferred_element_type=F32).astype(o_ref.dtype)


def sb_attention(q_raw, kv_raw, q_gain, k_gain, *, batch, seq, n_heads):
    t = q_raw.shape[0]
    return pl.pallas_call(
        _sb_kernel,
        grid=(batch, n_heads),
        in_specs=[pl.BlockSpec((seq, HEAD_DIM), lambda b, h: (b, h)),
                  pl.BlockSpec((seq, HEAD_DIM), lambda b, h: (b, h)),
                  pl.BlockSpec((seq, HEAD_DIM), lambda b, h: (b, n_heads + h)),
                  pl.BlockSpec((1, HEAD_DIM), lambda b, h: (0, 0)),
                  pl.BlockSpec((1, HEAD_DIM), lambda b, h: (0, 0))],
        out_specs=pl.BlockSpec((seq, HEAD_DIM), lambda b, h: (b, h)),
        out_shape=jax.ShapeDtypeStruct((t, n_heads * HEAD_DIM), BF16),
        scratch_shapes=[pltpu.VMEM((seq, HEAD_DIM), BF16),
                        pltpu.VMEM((seq, HEAD_DIM), BF16),
                        pltpu.VMEM((seq, HEAD_DIM), BF16),
                        pltpu.VMEM((SB_TILE, SB_TILE), BF16),
                        pltpu.VMEM((SB_TILE, seq), F32),
                        pltpu.VMEM((SB_TILE, seq), BF16)],
        compiler_params=_params("arbitrary", "arbitrary"),
        name="sb_attention",
    )(q_raw, kv_raw, kv_raw, q_gain.reshape(1, HEAD_DIM), k_gain.reshape(1, HEAD_DIM))


def _router_kernel(x_ref, g_ref, w_ref, b_ref, xn_ref, gate_ref, eid_ref):
    xn = _rms_rows(x_ref[...]) * g_ref[...]
    xn_ref[...] = xn
    x_hi = xn.astype(BF16)
    x_lo = (xn - x_hi.astype(F32)).astype(BF16)
    logits = jnp.dot(jnp.concatenate([x_hi, x_lo, x_hi], axis=1), w_ref[...],
                     preferred_element_type=F32) + b_ref[...]
    lane = lax.broadcasted_iota(jnp.int32, logits.shape, 1).astype(F32)
    neg = -jnp.inf
    first_lane = lambda hit: jnp.min(jnp.where(hit, lane, float(LANES)), axis=1, keepdims=True)
    lg = jnp.where(lane < MOE_GROUPS, logits, neg)
    mg = jnp.max(lg, axis=1, keepdims=True)
    g_val = 1.0 / jnp.sum(jnp.exp(lg - mg), axis=1, keepdims=True)
    g_idx = first_lane(lg == mg)
    lo = MOE_GROUPS + MOE_EXPERTS_PER_GROUP * g_idx
    le = jnp.where((lane >= lo) & (lane < lo + MOE_EXPERTS_PER_GROUP), logits, neg)
    m1 = jnp.max(le, axis=1, keepdims=True)
    i1 = first_lane(le == m1)
    le2 = jnp.where(lane == i1, neg, le)
    m2 = jnp.max(le2, axis=1, keepdims=True)
    i2 = first_lane(le2 == m2)
    den = jnp.sum(jnp.exp(le - m1), axis=1, keepdims=True)
    e1 = 1.0 / den
    e2 = jnp.exp(m2 - m1) / den
    w1 = g_val * e1 / (e1 + e2)
    w2 = g_val * e2 / (e1 + e2)
    gate_ref[...] = jnp.where(lane == 0, w1, jnp.where(lane == 1, w2, 0.0))
    eid = jnp.where(lane == 0, i1 - MOE_GROUPS, jnp.where(lane == 1, i2 - MOE_GROUPS, 0.0))
    eid_ref[...] = eid.astype(jnp.int32)


def moe_router(h, gain, w_rg, b_rg, w_re, b_re, *, tm=1024):
    t, d = h.shape
    n_logits = MOE_GROUPS + MOE_EXPERTS
    w = jnp.zeros((d, LANES), F32).at[:, :MOE_GROUPS].set(w_rg).at[:, MOE_GROUPS:n_logits].set(w_re)
    w_hi = w.astype(BF16)
    w_lo = (w - w_hi.astype(F32)).astype(BF16)
    w3 = jnp.concatenate([w_hi, w_hi, w_lo], axis=0)
    bias = jnp.zeros((1, LANES), F32).at[0, :MOE_GROUPS].set(b_rg).at[0, MOE_GROUPS:n_logits].set(b_re)
    return pl.pallas_call(
        _router_kernel,
        grid=(t // tm,),
        in_specs=[pl.BlockSpec((tm, d), lambda i: (i, 0)),
                  pl.BlockSpec((1, d), lambda i: (0, 0)),
                  pl.BlockSpec((3 * d, LANES), lambda i: (0, 0)),
                  pl.BlockSpec((1, LANES), lambda i: (0, 0))],
        out_specs=[pl.BlockSpec((tm, d), lambda i: (i, 0)),
                   pl.BlockSpec((tm, LANES), lambda i: (i, 0)),
                   pl.BlockSpec((tm, LANES), lambda i: (i, 0))],
        out_shape=[jax.ShapeDtypeStruct((t, d), F32),
                   jax.ShapeDtypeStruct((t, LANES), F32),
                   jax.ShapeDtypeStruct((t, LANES), jnp.int32)],
        compiler_params=_params("arbitrary"),
        name="moe_router",
    )(h, gain.reshape(1, d), w3, bias)


def _gather_rows(src_hbm, idx_ref, base, dst_ref, sem, n_rows, unrolled=False):
    def issue(r, carry):
        pltpu.make_async_copy(src_hbm.at[pl.ds(idx_ref[base + r], 1), :],
                              dst_ref.at[pl.ds(r, 1), :], sem).start()
        return carry
    if unrolled:
        for r in range(n_rows):
            issue(r, 0)
    else:
        lax.fori_loop(0, n_rows, issue, 0, unroll=8)


def _wait_rows(src_hbm, dst_ref, sem, n_rows):
    pltpu.make_async_copy(src_hbm.at[pl.ds(0, n_rows), :], dst_ref, sem).wait()


def _expert_kernel(blk_expert_ref, first_ref, next_ref, slot_tok_ref, n_used_ref,
                   x_hbm, wg_hbm, wu_hbm, wd_hbm, y_ref,
                   sg_ref, su_ref, sd_ref, wg_ref, wu_ref, wd_ref, xbuf_ref, wsem_ref, sem_ref, *, layer):
    i = pl.program_id(0)
    n_used = n_used_ref[0]
    depth = xbuf_ref.shape[0]
    slot = i % depth

    def weight_copies(e):
        copies = []
        for m, (w_hbm, stage_ref) in enumerate(((wg_hbm, sg_ref), (wu_hbm, su_ref), (wd_hbm, sd_ref))):
            slab = stage_ref.shape[0] // MOE_WEIGHT_SPLIT
            for q in range(MOE_WEIGHT_SPLIT):
                rows = pl.ds(q * slab, slab)
                copies.append(pltpu.make_async_copy(w_hbm.at[layer, e, rows, :], stage_ref.at[rows, :],
                                                    wsem_ref.at[m]))
        return copies

    def gather(blk, s, unrolled):
        _gather_rows(x_hbm, slot_tok_ref, blk * MOE_ROWS, xbuf_ref.at[s], sem_ref.at[s], MOE_ROWS, unrolled)

    def wait_gather(s):
        _wait_rows(x_hbm, xbuf_ref.at[s], sem_ref.at[s], MOE_ROWS)

    @pl.when((i == 0) & (n_used > 0))
    def _():
        for cp in weight_copies(blk_expert_ref[0]):
            cp.start()
        for blk in range(depth - 1):
            gather(blk, blk, False)

    is_first = (i < n_used) & (first_ref[i] == 1)

    @pl.when(is_first)
    def _():
        for cp in weight_copies(blk_expert_ref[i]):
            cp.wait()
        wg_ref[...] = sg_ref[...].astype(BF16)
        wu_ref[...] = su_ref[...].astype(BF16)
        wd_ref[...] = sd_ref[...].astype(BF16)

    @pl.when(is_first & (next_ref[i] >= 0))
    def _():
        for cp in weight_copies(next_ref[i]):
            cp.start(priority=1)

    @pl.when(i < n_used)
    def _():
        wait_gather(slot)
        xb = xbuf_ref[slot].astype(BF16)
        gather(i + depth - 1, (i + depth - 1) % depth, True)
        gate = jnp.dot(xb, wg_ref[...], preferred_element_type=F32)
        up = jnp.dot(xb, wu_ref[...], preferred_element_type=F32)
        hb = (_silu(gate) * up).astype(BF16)
        y_ref[...] = jnp.dot(hb, wd_ref[...], preferred_element_type=F32)

    @pl.when((i >= n_used) & (i < n_used + depth - 1) & (n_used > 0))
    def _():
        wait_gather(slot)

    @pl.when(i >= n_used)
    def _():
        y_ref[...] = jnp.zeros(y_ref.shape, F32)


def moe_experts(xn, blk_expert, first, nxt, slot_tok, n_used, w_gate, w_up, w_down, *, layer):
    t, d = xn.shape
    hidden = w_gate.shape[-1]
    n_steps = blk_expert.shape[0]
    n_blocks = n_steps - MOE_GATHER_DEPTH + 2
    grid_spec = pltpu.PrefetchScalarGridSpec(
        num_scalar_prefetch=5,
        grid=(n_steps,),
        in_specs=[pl.BlockSpec(memory_space=pl.ANY)] * 4,
        out_specs=pl.BlockSpec((MOE_ROWS, d), lambda i, *_: (jnp.minimum(i, n_blocks - 1), 0)),
        scratch_shapes=[pltpu.VMEM((d, hidden), F32),
                        pltpu.VMEM((d, hidden), F32),
                        pltpu.VMEM((hidden, d), F32),
                        pltpu.VMEM((d, hidden), BF16),
                        pltpu.VMEM((d, hidden), BF16),
                        pltpu.VMEM((hidden, d), BF16),
                        pltpu.VMEM((MOE_GATHER_DEPTH, MOE_ROWS, d), F32),
                        pltpu.SemaphoreType.DMA((3,)),
                        pltpu.SemaphoreType.DMA((MOE_GATHER_DEPTH,))],
    )
    return pl.pallas_call(
        functools.partial(_expert_kernel, layer=layer),
        grid_spec=grid_spec,
        out_shape=jax.ShapeDtypeStruct((n_blocks * MOE_ROWS, d), F32),
        compiler_params=_params("arbitrary"),
        name="moe_experts",
    )(blk_expert, first, nxt, slot_tok, n_used, xn, w_gate, w_up, w_down)


def _combine_kernel(dest0_ref, dest1_ref, h_ref, gate_ref, y_hbm, *rest, tm, emit_norm):
    o_ref = rest[0]
    buf_ref, sem_ref = rest[-2], rest[-1]
    i = pl.program_id(0)
    n = pl.num_programs(0)
    slot = i % 2

    def gather(step, s):
        _gather_rows(y_hbm, dest0_ref, step * tm, buf_ref.at[s, 0], sem_ref.at[s], tm)
        _gather_rows(y_hbm, dest1_ref, step * tm, buf_ref.at[s, 1], sem_ref.at[s], tm)

    @pl.when(i == 0)
    def _():
        gather(0, 0)

    @pl.when(i + 1 < n)
    def _():
        gather(i + 1, 1 - slot)

    _wait_rows(y_hbm, buf_ref.at[slot, 0], sem_ref.at[slot], tm)
    _wait_rows(y_hbm, buf_ref.at[slot, 1], sem_ref.at[slot], tm)
    gates = gate_ref[...]
    out = h_ref[...] + (gates[:, 0:1] * buf_ref[slot, 0] + gates[:, 1:2] * buf_ref[slot, 1])
    o_ref[...] = out
    if emit_norm:
        rest[1][...] = _rms_rows(out).astype(BF16)


def moe_combine(h, gates, ys, dest0, dest1, *, emit_norm, tm=256):
    t, d = h.shape
    out_specs = [pl.BlockSpec((tm, d), lambda i, d0, d1: (i, 0))]
    out_shape = [jax.ShapeDtypeStruct((t, d), F32)]
    if emit_norm:
        out_specs.append(pl.BlockSpec((tm, d), lambda i, d0, d1: (i, 0)))
        out_shape.append(jax.ShapeDtypeStruct((t, d), BF16))
    grid_spec = pltpu.PrefetchScalarGridSpec(
        num_scalar_prefetch=2,
        grid=(t // tm,),
        in_specs=[pl.BlockSpec((tm, d), lambda i, d0, d1: (i, 0)),
                  pl.BlockSpec((tm, LANES), lambda i, d0, d1: (i, 0)),
                  pl.BlockSpec(memory_space=pl.ANY)],
        out_specs=out_specs,
        scratch_shapes=[pltpu.VMEM((2, 2, tm, d), F32),
                        pltpu.SemaphoreType.DMA((2,))],
    )
    return pl.pallas_call(
        functools.partial(_combine_kernel, tm=tm, emit_norm=emit_norm),
        grid_spec=grid_spec,
        out_shape=out_shape,
        compiler_params=_params("arbitrary"),
        name="moe_combine",
    )(dest0, dest1, h, gates, ys)


def hierarchical_moe(h, gain, w_rg, b_rg, w_re, b_re, w_gate, w_up, w_down, *, layer, emit_norm):
    t, d = h.shape
    xn, gates, eids = moe_router(h, gain, w_rg, b_rg, w_re, b_re)
    eid = eids[:, :MOE_TOP_K].reshape(-1)
    n_assign = t * MOE_TOP_K
    experts = jnp.arange(MOE_EXPERTS, dtype=jnp.int32)
    onehot = (eid[:, None] == experts[None, :]).astype(jnp.int32)
    running = jnp.cumsum(onehot, axis=0)
    rank = jnp.sum(running * onehot, axis=1) - 1
    counts = running[-1]
    padded = (counts + MOE_ROWS - 1) // MOE_ROWS * MOE_ROWS
    pad_end = jnp.cumsum(padded)
    pad_start = pad_end - padded
    dest = (pad_start[eid] + rank).astype(jnp.int32)
    assert n_assign % MOE_ROWS == 0
    n_blocks = n_assign // MOE_ROWS + MOE_EXPERTS
    n_steps = n_blocks + MOE_GATHER_DEPTH - 2
    tok = jnp.arange(n_assign, dtype=jnp.int32) // MOE_TOP_K
    slot_tok = jnp.zeros((n_steps * MOE_ROWS,), jnp.int32).at[dest].set(
        tok, unique_indices=True, mode='promise_in_bounds')
    blk_start = jnp.arange(n_steps, dtype=jnp.int32) * MOE_ROWS
    blk_expert = jnp.minimum(jnp.sum((pad_end[None, :] <= blk_start[:, None]).astype(jnp.int32), axis=1),
                             MOE_EXPERTS - 1).astype(jnp.int32)
    n_used = (pad_end[-1:] // MOE_ROWS).astype(jnp.int32)
    first = (blk_start == pad_start[blk_expert]).astype(jnp.int32)
    later_nonempty = (experts[None, :] > experts[:, None]) & (counts[None, :] > 0)
    next_of_expert = jnp.min(jnp.where(later_nonempty, experts[None, :], MOE_EXPERTS), axis=1)
    next_of_expert = jnp.where(next_of_expert == MOE_EXPERTS, -1, next_of_expert).astype(jnp.int32)
    nxt = next_of_expert[blk_expert]
    ys = moe_experts(xn, blk_expert, first, nxt, slot_tok, n_used, w_gate, w_up, w_down, layer=layer)
    dest2 = dest.reshape(t, MOE_TOP_K)
    return moe_combine(h, gates, ys, dest2[:, 0], dest2[:, 1], emit_norm=emit_norm)


def kernel(x, norm_a, w_in_a, conv_a, a_log_a, dt_bias_a, out_norm_a, w_out_a, norm_kv, w_kv, k_norm_b, norm_b, w_q_b, q_norm_b, w_out_b, norm_moe, w_router_group, b_router_group, w_router_expert, b_router_expert, w_gate, w_up, w_down):
    batch, seq, d = x.shape
    depth = norm_moe.shape[0]
    n_a = norm_a.shape[0]
    n_kheads = d // HEAD_DIM
    kdim = n_kheads * HEAD_DIM
    n_main = 6 * kdim
    h = x.reshape(batch * seq, d)
    hn = None
    kv = None
    for layer in range(depth):
        if layer < n_a:
            i = layer
            w_in_t = jnp.swapaxes(w_in_a[i], 0, 1)
            xn, gates = gdn_gates(h, norm_a[i], w_in_t, n_main, a_log_a[i], dt_bias_a[i])
            proj = matmul(xn, w_in_t, n_cols=n_main, w_transposed=True, tm=1024, tn=1024)
            o = gdn_core(proj, gates, conv_a[i], out_norm_a[i], batch=batch, seq=seq, n_kheads=n_kheads)
            h = matmul(o, w_out_a[i], res=h, tm=1024, tn=512)
        else:
            i = layer - n_a
            if i == 0:
                kv = matmul(hn, w_kv, scale=norm_kv, tm=1024, tn=1024)
            q = matmul(hn, w_q_b[i], scale=norm_b[i], tm=1024, tn=1024)
            o = sb_attention(q, kv, q_norm_b[i], k_norm_b, batch=batch, seq=seq, n_heads=n_kheads)
            h = matmul(o, w_out_b[i], res=h, tm=1024, tn=1024)
        emit_norm = n_a <= layer + 1 < depth
        out = hierarchical_moe(h, norm_moe[layer], w_router_group[layer], b_router_group[layer],
                               w_router_expert[layer], b_router_expert[layer],
                               w_gate, w_up, w_down, layer=layer, emit_norm=emit_norm)
        h, hn = (out[0], out[1]) if emit_norm else (out[0], None)
    return h.reshape(batch, seq, d)
```

```python
import functools

import jax
import jax.numpy as jnp
from jax import lax
from jax.experimental import pallas as pl
from jax.experimental.pallas import tpu as pltpu

NORM_EPS = 1e-6
LOG2E = 1.4426950408889634
HEAD_DIM = 128
GDN_CONV = 4
GDN_CHUNK = 64
GDN_GROUP = 16
SB_TILE = 256
MOE_GROUPS = 4
MOE_EXPERTS_PER_GROUP = 8
MOE_EXPERTS = MOE_GROUPS * MOE_EXPERTS_PER_GROUP
MOE_TOP_K = 2
MOE_ROWS = 256
MOE_GATHER_DEPTH = 3
MOE_WEIGHT_SPLIT = 4
_ISSUE_UNROLL = 8
LANES = 128
VMEM_LIMIT = 56 * 1024 * 1024

BF16 = jnp.bfloat16
F32 = jnp.float32


def _params(*semantics):
    return pltpu.CompilerParams(dimension_semantics=semantics, vmem_limit_bytes=VMEM_LIMIT)


def _silu(x):
    return x * (1.0 / (1.0 + jnp.exp(-x)))


def _softplus(x):
    return jnp.maximum(x, 0.0) + jnp.log(1.0 + jnp.exp(-jnp.abs(x)))


def _rms_rows(x):
    return x * lax.rsqrt(jnp.mean(x * x, axis=-1, keepdims=True) + NORM_EPS)


def _bdot(a, b):
    return jnp.einsum('nik,nkj->nij', a.astype(BF16), b.astype(BF16), preferred_element_type=F32)


def _matmul_kernel(*refs, has_scale, has_res, w_rows_are_outputs):
    a_ref, w_ref = refs[0], refs[1]
    scale_ref = refs[2] if has_scale else None
    res_ref = refs[2 + has_scale] if has_res else None
    o_ref, wb_ref = refs[-2], refs[-1]

    @pl.when(pl.program_id(1) == 0)
    def _():
        w = w_ref[...]
        if has_scale:
            w = w * scale_ref[...]
        wb_ref[...] = w.astype(BF16)

    w_contract = 1 if w_rows_are_outputs else 0
    acc = lax.dot_general(a_ref[...], wb_ref[...], (((1,), (w_contract,)), ((), ())),
                          preferred_element_type=F32)
    if has_res:
        acc = res_ref[...] + acc
    o_ref[...] = acc.astype(o_ref.dtype)


def matmul(a, w, *, n_cols=None, scale=None, res=None, w_transposed=False, tm, tn, out_dtype=F32):
    t, k = a.shape
    n = w.shape[0 if w_transposed else 1] if n_cols is None else n_cols
    tn = min(tn, n)
    while n % tn:
        tn //= 2
    assert t % tm == 0 and tn % LANES == 0, (t, tm, n, tn)
    assert not (w_transposed and scale is not None)
    w_block = (tn, k) if w_transposed else (k, tn)
    w_index = (lambda j, i: (j, 0)) if w_transposed else (lambda j, i: (0, j))
    in_specs = [pl.BlockSpec((tm, k), lambda j, i: (i, 0)),
                pl.BlockSpec(w_block, w_index)]
    args = [a, w]
    if scale is not None:
        in_specs.append(pl.BlockSpec((k, 1), lambda j, i: (0, 0)))
        args.append(scale.reshape(k, 1))
    if res is not None:
        in_specs.append(pl.BlockSpec((tm, tn), lambda j, i: (i, j)))
        args.append(res)
    return pl.pallas_call(
        functools.partial(_matmul_kernel, has_scale=scale is not None, has_res=res is not None,
                          w_rows_are_outputs=w_transposed),
        grid=(n // tn, t // tm),
        in_specs=in_specs,
        out_specs=pl.BlockSpec((tm, tn), lambda j, i: (i, j)),
        out_shape=jax.ShapeDtypeStruct((t, n), out_dtype),
        scratch_shapes=[pltpu.VMEM(w_block, BF16)],
        compiler_params=_params("arbitrary", "arbitrary"),
        name="matmul",
    )(*args)


def _gdn_gates_kernel(x_ref, g_ref, w_hbm, alog_ref, dt_ref, xn_ref, o_ref, ws_ref, wb_ref, sem_ref, *,
                      n_heads, col0):
    @pl.when(pl.program_id(0) == 0)
    def _():
        cp = pltpu.make_async_copy(w_hbm.at[pl.ds(col0, 2 * n_heads), :], ws_ref, sem_ref)
        cp.start()
        cp.wait()
        wb_ref[...] = jnp.zeros(wb_ref.shape, BF16)
        wb_ref[0:2 * n_heads, :] = ws_ref[...].astype(BF16)

    xn = (_rms_rows(x_ref[...]) * g_ref[...]).astype(BF16)
    xn_ref[...] = xn
    y = lax.dot_general(xn, wb_ref[...], (((1,), (1,)), ((), ())),
                        preferred_element_type=F32)
    lane = lax.broadcasted_iota(jnp.int32, y.shape, 1)
    row = lax.broadcasted_iota(jnp.int32, y.shape, 0)
    beta = 1.0 / (1.0 + jnp.exp(-y))
    g = -jnp.exp(alog_ref[...]) * _softplus(y + dt_ref[...])
    pos = row & (GDN_CHUNK - 1)
    shift = 1
    while shift < GDN_CHUNK:
        g = g + jnp.where(pos >= shift, pltpu.roll(g, shift, axis=0), 0.0)
        shift *= 2
    o_ref[...] = jnp.where(lane < n_heads, beta, jnp.where(lane < 2 * n_heads, g, 0.0))


def gdn_gates(x, gain, w_in_t, col0, a_log, dt_bias, *, tm=512):
    t, d = x.shape
    hv = a_log.shape[0]
    row_param = lambda p: jnp.zeros((1, LANES), F32).at[0, hv:2 * hv].set(p.astype(F32))
    return pl.pallas_call(
        functools.partial(_gdn_gates_kernel, n_heads=hv, col0=col0),
        grid=(t // tm,),
        in_specs=[pl.BlockSpec((tm, d), lambda i: (i, 0)),
                  pl.BlockSpec((1, d), lambda i: (0, 0)),
                  pl.BlockSpec(memory_space=pl.ANY),
                  pl.BlockSpec((1, LANES), lambda i: (0, 0)),
                  pl.BlockSpec((1, LANES), lambda i: (0, 0))],
        out_specs=[pl.BlockSpec((tm, d), lambda i: (i, 0)),
                   pl.BlockSpec((tm, LANES), lambda i: (i, 0))],
        out_shape=[jax.ShapeDtypeStruct((t, d), BF16),
                   jax.ShapeDtypeStruct((t, LANES), F32)],
        scratch_shapes=[pltpu.VMEM((2 * hv, d), F32),
                        pltpu.VMEM((LANES, d), BF16),
                        pltpu.SemaphoreType.DMA(())],
        compiler_params=_params("arbitrary"),
        name="gdn_gates",
    )(x, gain.reshape(1, d), w_in_t, row_param(a_log), row_param(dt_bias))


def _unit_lower_inverse(low):
    c = low.shape[-1]
    ri = lax.broadcasted_iota(jnp.int32, low.shape, 1)
    ci = lax.broadcasted_iota(jnp.int32, low.shape, 2)
    eye = jnp.where(ri == ci, 1.0, 0.0).astype(F32)
    x = eye
    s = 1
    while s < c:
        join = ((ri & -(2 * s)) == (ci & -(2 * s))) & ((ri & s) != 0) & ((ci & s) == 0)
        cs = jnp.where(join, low, 0.0)
        if s == 1:
            x = eye - cs
        else:
            xb = x.astype(BF16)
            x = x - _bdot(xb, _bdot(cs, xb))
        s *= 2
    return x


def _gdn_kernel(q_ref, k_ref, v_ref, z_ref, gb_ref, gct_ref, cq_ref, ck_ref, cv_ref, gain_ref, o_ref,
                pad_ref, qs_ref, ks_ref, kt_ref, vs_ref, bcol_ref, gcol_ref,
                pq_ref, n_ref, op_ref, d_ref, *, n_vheads):
    seq = q_ref.shape[0]
    n_chunks = seq // GDN_CHUNK
    c = GDN_CHUNK
    grp_chunks = GDN_GROUP
    h = pl.program_id(1)

    grp_rows = grp_chunks * c
    pad_ref[0:8, :] = jnp.zeros((8, pad_ref.shape[1]), F32)

    def l2n(x):
        return x * lax.rsqrt(jnp.sum(x * x, axis=-1, keepdims=True) + NORM_EPS)

    def prep(grp):
        r0 = grp * grp_rows
        sl = slice(r0, r0 + grp_rows)

        def conv_silu(x_ref, cw_ref, col0, width):
            cols = slice(col0, col0 + width)
            pad_ref[8 + r0:8 + r0 + grp_rows, cols] = x_ref[sl, :]
            y = pad_ref[5 + r0:5 + r0 + grp_rows, cols] * cw_ref[0:1, :]
            for j in range(1, GDN_CONV):
                y = y + pad_ref[5 + j + r0:5 + j + r0 + grp_rows, cols] * cw_ref[j:j + 1, :]
            return _silu(y)

        qs_ref[sl, :] = l2n(conv_silu(q_ref, cq_ref, 0, HEAD_DIM)) * (HEAD_DIM ** -0.5)
        kk_ = l2n(conv_silu(k_ref, ck_ref, HEAD_DIM, HEAD_DIM))
        ks_ref[sl, :] = kk_
        vs_ref[sl, :] = conv_silu(v_ref, cv_ref, 2 * HEAD_DIM, 2 * HEAD_DIM)
        for ci_ in range(grp_chunks):
            kt_ref[grp * grp_chunks + ci_] = kk_[ci_ * c:(ci_ + 1) * c, :].T.astype(BF16)
        gb = gb_ref[sl, :]
        lane = lax.broadcasted_iota(jnp.int32, gb.shape, 1)
        for r in range(2):
            hv = 2 * h + r
            bcol = jnp.sum(jnp.where(lane == hv, gb, 0.0), axis=1, keepdims=True)
            gcol = jnp.sum(jnp.where(lane == n_vheads + hv, gb, 0.0), axis=1, keepdims=True)
            bcol_ref[r, sl, :] = jnp.broadcast_to(bcol, (grp_rows, HEAD_DIM))
            gcol_ref[r, sl, :] = jnp.broadcast_to(gcol, (grp_rows, HEAD_DIM))

    def phase1(grp):
        g = grp_chunks
        rows = slice(grp * grp_rows, (grp + 1) * grp_rows)
        chunks = slice(grp * g, (grp + 1) * g)
        q3 = qs_ref[rows, :].reshape(g, c, HEAD_DIM)
        k3 = ks_ref[rows, :].reshape(g, c, HEAD_DIM)
        kt = kt_ref[chunks]
        qkk = _bdot(jnp.concatenate([q3, k3], axis=1), kt)
        qk, kk = qkk[:, :c], qkk[:, c:]
        ri = lax.broadcasted_iota(jnp.int32, (g, c, c), 1)
        ci = lax.broadcasted_iota(jnp.int32, (g, c, c), 2)
        incl = ri >= ci
        lows, rhss, attns, kdts, qgs = [], [], [], [], []
        for r in range(2):
            b = bcol_ref[r, rows, :].reshape(g, c, HEAD_DIM)
            gcf = gcol_ref[r, rows, :].reshape(g, c, HEAD_DIM)
            grow = gct_ref[0, r, chunks]
            glast = gcf[:, c - 1:c, :]
            diff = gcf[:, :, :c] - grow
            decay = jnp.where(incl, jnp.exp(jnp.where(incl, diff, 0.0)), 0.0)
            lows.append(jnp.where(ri > ci, b[:, :, :c] * kk * decay, 0.0))
            attns.append(qk * decay)
            egc = jnp.exp(gcf)
            v3 = vs_ref[rows, r * HEAD_DIM:(r + 1) * HEAD_DIM].reshape(g, c, HEAD_DIM)
            rhss.append(jnp.concatenate([v3 * b, k3 * b * egc], axis=2))
            kdts.append(kt.astype(F32) * jnp.exp(glast[:, :, :c] - grow))
            qgs.append(q3 * egc)
            d_ref[r, chunks] = jnp.exp(glast)
        tinv = _unit_lower_inverse(jnp.concatenate(lows, axis=0))
        uw = _bdot(tinv, jnp.concatenate(rhss, axis=0)).astype(BF16)
        auw = _bdot(jnp.concatenate(attns, axis=0), uw)
        kuw = _bdot(jnp.concatenate(kdts, axis=0), uw)
        for r in range(2):
            sel = slice(r * g, (r + 1) * g)
            pq_ref[r, chunks] = jnp.concatenate(
                [kuw[sel, :, HEAD_DIM:], qgs[r] - auw[sel, :, HEAD_DIM:]], axis=1).astype(BF16)
            n_ref[r, chunks] = kuw[sel, :, :HEAD_DIM]
            op_ref[r, rows, :] = auw[sel, :, :HEAD_DIM].reshape(g * c, HEAD_DIM)

    for grp in range(n_chunks // grp_chunks):
        prep(grp)
        phase1(grp)

    gain = gain_ref[...]
    state = jnp.zeros((2, HEAD_DIM, HEAD_DIM), F32)
    for ch in range(n_chunks):
        rows = slice(ch * c, (ch + 1) * c)
        ps = _bdot(pq_ref[:, ch], state)
        state = state * d_ref[:, ch] - ps[:, :HEAD_DIM] + n_ref[:, ch]
        o = op_ref[:, rows, :] + ps[:, HEAD_DIM:]
        for r in range(2):
            zc = z_ref[rows, r * HEAD_DIM:(r + 1) * HEAD_DIM]
            o_ref[rows, r * HEAD_DIM:(r + 1) * HEAD_DIM] = (_rms_rows(o[r]) * gain * _silu(zc)).astype(o_ref.dtype)


def gdn_core(proj, gates, conv_w, out_gain, *, batch, seq, n_kheads):
    t = proj.shape[0]
    n_vheads = 2 * n_kheads
    kdim = n_kheads * HEAD_DIM
    vdim = 2 * kdim
    n_chunks = seq // GDN_CHUNK
    c = GDN_CHUNK
    gct = gates[:, n_vheads:2 * n_vheads].reshape(batch, n_chunks, c, n_kheads, 2)
    gct = gct.transpose(0, 3, 4, 1, 2).reshape(batch * n_kheads, 2, n_chunks, 1, c)
    kq = kdim // HEAD_DIM
    return pl.pallas_call(
        functools.partial(_gdn_kernel, n_vheads=n_vheads),
        grid=(batch, n_kheads),
        in_specs=[pl.BlockSpec((seq, HEAD_DIM), lambda b, h: (b, h)),
                  pl.BlockSpec((seq, HEAD_DIM), lambda b, h: (b, kq + h)),
                  pl.BlockSpec((seq, 2 * HEAD_DIM), lambda b, h: (b, kq + h)),
                  pl.BlockSpec((seq, 2 * HEAD_DIM), lambda b, h: (b, 2 * kq + h)),
                  pl.BlockSpec((seq, LANES), lambda b, h: (b, 0)),
                  pl.BlockSpec((1, 2, n_chunks, 1, c), lambda b, h: (b * n_kheads + h, 0, 0, 0, 0)),
                  pl.BlockSpec((GDN_CONV, HEAD_DIM), lambda b, h: (0, h)),
                  pl.BlockSpec((GDN_CONV, HEAD_DIM), lambda b, h: (0, kq + h)),
                  pl.BlockSpec((GDN_CONV, 2 * HEAD_DIM), lambda b, h: (0, kq + h)),
                  pl.BlockSpec((1, HEAD_DIM), lambda b, h: (0, 0))],
        out_specs=pl.BlockSpec((seq, 2 * HEAD_DIM), lambda b, h: (b, h)),
        out_shape=jax.ShapeDtypeStruct((t, vdim), BF16),
        scratch_shapes=[pltpu.VMEM((seq + 8, 4 * HEAD_DIM), F32),
                        pltpu.VMEM((seq, HEAD_DIM), F32),
                        pltpu.VMEM((seq, HEAD_DIM), F32),
                        pltpu.VMEM((n_chunks, HEAD_DIM, c), BF16),
                        pltpu.VMEM((seq, 2 * HEAD_DIM), F32),
                        pltpu.VMEM((2, seq, HEAD_DIM), F32),
                        pltpu.VMEM((2, seq, HEAD_DIM), F32),
                        pltpu.VMEM((2, n_chunks, HEAD_DIM + c, HEAD_DIM), BF16),
                        pltpu.VMEM((2, n_chunks, HEAD_DIM, HEAD_DIM), F32),
                        pltpu.VMEM((2, seq, HEAD_DIM), F32),
                        pltpu.VMEM((2, n_chunks, 1, HEAD_DIM), F32)],
        compiler_params=_params("arbitrary", "arbitrary"),
        name="gdn_core",
    )(proj, proj, proj, proj, gates, gct, conv_w, conv_w, conv_w, out_gain.reshape(1, HEAD_DIM))


def _sb_kernel(q_ref, k_ref, v_ref, qg_ref, kg_ref, o_ref, qs_ref, ks_ref, vs_ref, tri_ref, z_ref, p_ref):
    seq = q_ref.shape[0]
    t = SB_TILE
    qs_ref[...] = (_rms_rows(q_ref[...]) * qg_ref[...] * (-LOG2E * HEAD_DIM ** -0.5)).astype(BF16)
    ks_ref[...] = (_rms_rows(k_ref[...]) * kg_ref[...]).astype(BF16)
    vs_ref[...] = v_ref[...].astype(BF16)
    ri = lax.broadcasted_iota(jnp.int32, (t, t), 0)
    ci = lax.broadcasted_iota(jnp.int32, (t, t), 1)
    tri_ref[...] = jnp.where(ri > ci, 1.0, 0.0).astype(BF16)
    causal = ci < ri

    for qi in range(seq // t):
        width = (qi + 1) * t
        z_ref[:, 0:width] = lax.dot_general(qs_ref[qi * t:(qi + 1) * t, :], ks_ref[0:width, :],
                                            (((1,), (1,)), ((), ())), preferred_element_type=F32)
        later0 = jnp.zeros((t, 1), F32)
        for s in range(qi, -1, -1):
            zn = z_ref[:, s * t:(s + 1) * t]
            lf = jnp.minimum(zn, 0.0) - jnp.log(1.0 + jnp.exp2(-jnp.abs(zn))) * LOG2E
            if s == qi:
                lf = jnp.where(causal, lf, 0.0)
            later = jnp.dot(lf, tri_ref[...].astype(F32), preferred_element_type=F32) + later0
            wgt = jnp.exp2((lf - zn) + later)
            if s == qi:
                wgt = jnp.where(causal, wgt, 0.0)
            p_ref[:, s * t:(s + 1) * t] = wgt.astype(BF16)
            later0 = later0 + jnp.sum(lf, axis=1, keepdims=True)
        o_ref[qi * t:(qi + 1) * t, :] = jnp.dot(p_ref[:, 0:width], vs_ref[0:width, :],
                                                preferred_element_type=F32).astype(o_ref.dtype)


def sb_attention(q_raw, kv_raw, q_gain, k_gain, *, batch, seq, n_heads):
    t = q_raw.shape[0]
    return pl.pallas_call(
        _sb_kernel,
        grid=(batch, n_heads),
        in_specs=[pl.BlockSpec((seq, HEAD_DIM), lambda b, h: (b, h)),
                  pl.BlockSpec((seq, HEAD_DIM), lambda b, h: (b, h)),
                  pl.BlockSpec((seq, HEAD_DIM), lambda b, h: (b, n_heads + h)),
                  pl.BlockSpec((1, HEAD_DIM), lambda b, h: (0, 0)),
                  pl.BlockSpec((1, HEAD_DIM), lambda b, h: (0, 0))],
        out_specs=pl.BlockSpec((seq, HEAD_DIM), lambda b, h: (b, h)),
        out_shape=jax.ShapeDtypeStruct((t, n_heads * HEAD_DIM), BF16),
        scratch_shapes=[pltpu.VMEM((seq, HEAD_DIM), BF16),
                        pltpu.VMEM((seq, HEAD_DIM), BF16),
                        pltpu.VMEM((seq, HEAD_DIM), BF16),
                        pltpu.VMEM((SB_TILE, SB_TILE), BF16),
                        pltpu.VMEM((SB_TILE, seq), F32),
                        pltpu.VMEM((SB_TILE, seq), BF16)],
        compiler_params=_params("arbitrary", "arbitrary"),
        name="sb_attention",
    )(q_raw, kv_raw, kv_raw, q_gain.reshape(1, HEAD_DIM), k_gain.reshape(1, HEAD_DIM))


def _router_kernel(x_ref, g_ref, w_ref, b_ref, xn_ref, gate_ref, eid_ref):
    xn = _rms_rows(x_ref[...]) * g_ref[...]
    xn_ref[...] = xn
    x_hi = xn.astype(BF16)
    x_lo = (xn - x_hi.astype(F32)).astype(BF16)
    logits = jnp.dot(jnp.concatenate([x_hi, x_lo, x_hi], axis=1), w_ref[...],
                     preferred_element_type=F32) + b_ref[...]
    lane = lax.broadcasted_iota(jnp.int32, logits.shape, 1).astype(F32)
    neg = -jnp.inf
    first_lane = lambda hit: jnp.min(jnp.where(hit, lane, float(LANES)), axis=1, keepdims=True)
    lg = jnp.where(lane < MOE_GROUPS, logits, neg)
    mg = jnp.max(lg, axis=1, keepdims=True)
    g_val = 1.0 / jnp.sum(jnp.exp(lg - mg), axis=1, keepdims=True)
    g_idx = first_lane(lg == mg)
    lo = MOE_GROUPS + MOE_EXPERTS_PER_GROUP * g_idx
    le = jnp.where((lane >= lo) & (lane < lo + MOE_EXPERTS_PER_GROUP), logits, neg)
    m1 = jnp.max(le, axis=1, keepdims=True)
    i1 = first_lane(le == m1)
    le2 = jnp.where(lane == i1, neg, le)
    m2 = jnp.max(le2, axis=1, keepdims=True)
    i2 = first_lane(le2 == m2)
    den = jnp.sum(jnp.exp(le - m1), axis=1, keepdims=True)
    e1 = 1.0 / den
    e2 = jnp.exp(m2 - m1) / den
    w1 = g_val * e1 / (e1 + e2)
    w2 = g_val * e2 / (e1 + e2)
    gate_ref[...] = jnp.where(lane == 0, w1, jnp.where(lane == 1, w2, 0.0))
    eid = jnp.where(lane == 0, i1 - MOE_GROUPS, jnp.where(lane == 1, i2 - MOE_GROUPS, 0.0))
    eid_ref[...] = eid.astype(jnp.int32)


def moe_router(h, gain, w_rg, b_rg, w_re, b_re, *, tm=1024):
    t, d = h.shape
    n_logits = MOE_GROUPS + MOE_EXPERTS
    w = jnp.zeros((d, LANES), F32).at[:, :MOE_GROUPS].set(w_rg).at[:, MOE_GROUPS:n_logits].set(w_re)
    w_hi = w.astype(BF16)
    w_lo = (w - w_hi.astype(F32)).astype(BF16)
    w3 = jnp.concatenate([w_hi, w_hi, w_lo], axis=0)
    bias = jnp.zeros((1, LANES), F32).at[0, :MOE_GROUPS].set(b_rg).at[0, MOE_GROUPS:n_logits].set(b_re)
    return pl.pallas_call(
        _router_kernel,
        grid=(t // tm,),
        in_specs=[pl.BlockSpec((tm, d), lambda i: (i, 0)),
                  pl.BlockSpec((1, d), lambda i: (0, 0)),
                  pl.BlockSpec((3 * d, LANES), lambda i: (0, 0)),
                  pl.BlockSpec((1, LANES), lambda i: (0, 0))],
        out_specs=[pl.BlockSpec((tm, d), lambda i: (i, 0)),
                   pl.BlockSpec((tm, LANES), lambda i: (i, 0)),
                   pl.BlockSpec((tm, LANES), lambda i: (i, 0))],
        out_shape=[jax.ShapeDtypeStruct((t, d), F32),
                   jax.ShapeDtypeStruct((t, LANES), F32),
                   jax.ShapeDtypeStruct((t, LANES), jnp.int32)],
        compiler_params=_params("arbitrary"),
        name="moe_router",
    )(h, gain.reshape(1, d), w3, bias)


def _gather_rows(src_hbm, idx_ref, base, dst_ref, sem, n_rows, unrolled=False, both_priorities=False):
    def issue(r, carry, priority=0):
        pltpu.make_async_copy(src_hbm.at[pl.ds(idx_ref[base + r], 1), :],
                              dst_ref.at[pl.ds(r, 1), :], sem).start(priority=priority)
        return carry
    if unrolled:
        for r in range(n_rows):
            issue(r, 0)
    elif both_priorities:
        def issue_group(g, carry):
            for j in range(_ISSUE_UNROLL):
                issue(g * _ISSUE_UNROLL + j, carry, priority=j % 2)
            return carry
        lax.fori_loop(0, n_rows // _ISSUE_UNROLL, issue_group, 0)
    else:
        lax.fori_loop(0, n_rows, issue, 0, unroll=_ISSUE_UNROLL)


def _wait_rows(src_hbm, dst_ref, sem, n_rows):
    pltpu.make_async_copy(src_hbm.at[pl.ds(0, n_rows), :], dst_ref, sem).wait()


def _expert_kernel(blk_expert_ref, first_ref, next_ref, slot_tok_ref, n_used_ref,
                   x_hbm, wg_hbm, wu_hbm, wd_hbm, y_ref,
                   sg_ref, su_ref, sd_ref, wg_ref, wu_ref, wd_ref, xbuf_ref, wsem_ref, sem_ref, *, layer):
    i = pl.program_id(0)
    n_used = n_used_ref[0]
    depth = xbuf_ref.shape[0]
    slot = i % depth

    def weight_copies(e):
        copies = []
        for m, (w_hbm, stage_ref) in enumerate(((wg_hbm, sg_ref), (wu_hbm, su_ref), (wd_hbm, sd_ref))):
            slab = stage_ref.shape[0] // MOE_WEIGHT_SPLIT
            for q in range(MOE_WEIGHT_SPLIT):
                rows = pl.ds(q * slab, slab)
                copies.append(pltpu.make_async_copy(w_hbm.at[layer, e, rows, :], stage_ref.at[rows, :],
                                                    wsem_ref.at[m]))
        return copies

    def gather(blk, s, unrolled):
        _gather_rows(x_hbm, slot_tok_ref, blk * MOE_ROWS, xbuf_ref.at[s], sem_ref.at[s], MOE_ROWS, unrolled)

    def wait_gather(s):
        _wait_rows(x_hbm, xbuf_ref.at[s], sem_ref.at[s], MOE_ROWS)

    @pl.when((i == 0) & (n_used > 0))
    def _():
        for cp in weight_copies(blk_expert_ref[0]):
            cp.start()
        for blk in range(depth - 1):
            gather(blk, blk, False)

    is_first = (i < n_used) & (first_ref[i] == 1)

    @pl.when(is_first)
    def _():
        for cp in weight_copies(blk_expert_ref[i]):
            cp.wait()
        wg_ref[...] = sg_ref[...].astype(BF16)
        wu_ref[...] = su_ref[...].astype(BF16)
        wd_ref[...] = sd_ref[...].astype(BF16)

    @pl.when(is_first & (next_ref[i] >= 0))
    def _():
        for cp in weight_copies(next_ref[i]):
            cp.start(priority=1)

    @pl.when(i < n_used)
    def _():
        wait_gather(slot)
        xb = xbuf_ref[slot].astype(BF16)
        gather(i + depth - 1, (i + depth - 1) % depth, True)
        gate = jnp.dot(xb, wg_ref[...], preferred_element_type=F32)
        up = jnp.dot(xb, wu_ref[...], preferred_element_type=F32)
        hb = (_silu(gate) * up).astype(BF16)
        y_ref[...] = jnp.dot(hb, wd_ref[...], preferred_element_type=F32)

    @pl.when((i >= n_used) & (i < n_used + depth - 1) & (n_used > 0))
    def _():
        wait_gather(slot)

    @pl.when(i >= n_used)
    def _():
        y_ref[...] = jnp.zeros(y_ref.shape, F32)


def moe_experts(xn, blk_expert, first, nxt, slot_tok, n_used, w_gate, w_up, w_down, *, layer):
    t, d = xn.shape
    hidden = w_gate.shape[-1]
    n_steps = blk_expert.shape[0]
    n_blocks = n_steps - MOE_GATHER_DEPTH + 2
    grid_spec = pltpu.PrefetchScalarGridSpec(
        num_scalar_prefetch=5,
        grid=(n_steps,),
        in_specs=[pl.BlockSpec(memory_space=pl.ANY)] * 4,
        out_specs=pl.BlockSpec((MOE_ROWS, d), lambda i, *_: (jnp.minimum(i, n_blocks - 1), 0)),
        scratch_shapes=[pltpu.VMEM((d, hidden), F32),
                        pltpu.VMEM((d, hidden), F32),
                        pltpu.VMEM((hidden, d), F32),
                        pltpu.VMEM((d, hidden), BF16),
                        pltpu.VMEM((d, hidden), BF16),
                        pltpu.VMEM((hidden, d), BF16),
                        pltpu.VMEM((MOE_GATHER_DEPTH, MOE_ROWS, d), F32),
                        pltpu.SemaphoreType.DMA((3,)),
                        pltpu.SemaphoreType.DMA((MOE_GATHER_DEPTH,))],
    )
    return pl.pallas_call(
        functools.partial(_expert_kernel, layer=layer),
        grid_spec=grid_spec,
        out_shape=jax.ShapeDtypeStruct((n_blocks * MOE_ROWS, d), F32),
        compiler_params=_params("arbitrary"),
        name="moe_experts",
    )(blk_expert, first, nxt, slot_tok, n_used, xn, w_gate, w_up, w_down)


def _combine_kernel(dest0_ref, dest1_ref, h_ref, gate_ref, y_hbm, *rest, tm, emit_norm):
    o_ref = rest[0]
    buf_ref, sem_ref = rest[-2], rest[-1]
    i = pl.program_id(0)
    n = pl.num_programs(0)
    slot = i % 2

    def gather(step, s):
        _gather_rows(y_hbm, dest0_ref, step * tm, buf_ref.at[s, 0], sem_ref.at[s], tm, both_priorities=True)
        _gather_rows(y_hbm, dest1_ref, step * tm, buf_ref.at[s, 1], sem_ref.at[s], tm, both_priorities=True)

    @pl.when(i == 0)
    def _():
        gather(0, 0)

    @pl.when(i + 1 < n)
    def _():
        gather(i + 1, 1 - slot)

    _wait_rows(y_hbm, buf_ref.at[slot, 0], sem_ref.at[slot], tm)
    _wait_rows(y_hbm, buf_ref.at[slot, 1], sem_ref.at[slot], tm)
    gates = gate_ref[...]
    out = h_ref[...] + (gates[:, 0:1] * buf_ref[slot, 0] + gates[:, 1:2] * buf_ref[slot, 1])
    o_ref[...] = out
    if emit_norm:
        rest[1][...] = _rms_rows(out).astype(BF16)


def moe_combine(h, gates, ys, dest0, dest1, *, emit_norm, tm=256):
    t, d = h.shape
    out_specs = [pl.BlockSpec((tm, d), lambda i, d0, d1: (i, 0))]
    out_shape = [jax.ShapeDtypeStruct((t, d), F32)]
    if emit_norm:
        out_specs.append(pl.BlockSpec((tm, d), lambda i, d0, d1: (i, 0)))
        out_shape.append(jax.ShapeDtypeStruct((t, d), BF16))
    grid_spec = pltpu.PrefetchScalarGridSpec(
        num_scalar_prefetch=2,
        grid=(t // tm,),
        in_specs=[pl.BlockSpec((tm, d), lambda i, d0, d1: (i, 0)),
                  pl.BlockSpec((tm, LANES), lambda i, d0, d1: (i, 0)),
                  pl.BlockSpec(memory_space=pl.ANY)],
        out_specs=out_specs,
        scratch_shapes=[pltpu.VMEM((2, 2, tm, d), F32),
                        pltpu.SemaphoreType.DMA((2,))],
    )
    return pl.pallas_call(
        functools.partial(_combine_kernel, tm=tm, emit_norm=emit_norm),
        grid_spec=grid_spec,
        out_shape=out_shape,
        compiler_params=_params("arbitrary"),
        name="moe_combine",
    )(dest0, dest1, h, gates, ys)


def hierarchical_moe(h, gain, w_rg, b_rg, w_re, b_re, w_gate, w_up, w_down, *, layer, emit_norm):
    t, d = h.shape
    xn, gates, eids = moe_router(h, gain, w_rg, b_rg, w_re, b_re)
    eid = eids[:, :MOE_TOP_K].reshape(-1)
    n_assign = t * MOE_TOP_K
    experts = jnp.arange(MOE_EXPERTS, dtype=jnp.int32)
    onehot = (eid[:, None] == experts[None, :]).astype(jnp.int32)
    running = jnp.cumsum(onehot, axis=0)
    counts = running[-1]
    padded = (counts + MOE_ROWS - 1) // MOE_ROWS * MOE_ROWS
    pad_end = jnp.cumsum(padded)
    pad_start = pad_end - padded
    dest = jnp.sum(onehot * (running - 1 + pad_start[None, :]), axis=1).astype(jnp.int32)
    assert n_assign % MOE_ROWS == 0
    n_blocks = n_assign // MOE_ROWS + MOE_EXPERTS
    n_steps = n_blocks + MOE_GATHER_DEPTH - 2
    tok = jnp.arange(n_assign, dtype=jnp.int32) // MOE_TOP_K
    slot_tok = jnp.zeros((n_steps * MOE_ROWS,), jnp.int32).at[dest].set(
        tok, unique_indices=True, mode='promise_in_bounds')
    blk_start = jnp.arange(n_steps, dtype=jnp.int32) * MOE_ROWS
    blk_expert = jnp.minimum(jnp.sum((pad_end[None, :] <= blk_start[:, None]).astype(jnp.int32), axis=1),
                             MOE_EXPERTS - 1).astype(jnp.int32)
    n_used = (pad_end[-1:] // MOE_ROWS).astype(jnp.int32)
    first = (blk_start == pad_start[blk_expert]).astype(jnp.int32)
    later_nonempty = (experts[None, :] > experts[:, None]) & (counts[None, :] > 0)
    next_of_expert = jnp.min(jnp.where(later_nonempty, experts[None, :], MOE_EXPERTS), axis=1)
    next_of_expert = jnp.where(next_of_expert == MOE_EXPERTS, -1, next_of_expert).astype(jnp.int32)
    nxt = next_of_expert[blk_expert]
    ys = moe_experts(xn, blk_expert, first, nxt, slot_tok, n_used, w_gate, w_up, w_down, layer=layer)
    dest2 = dest.reshape(t, MOE_TOP_K)
    return moe_combine(h, gates, ys, dest2[:, 0], dest2[:, 1], emit_norm=emit_norm)


def kernel(x, norm_a, w_in_a, conv_a, a_log_a, dt_bias_a, out_norm_a, w_out_a, norm_kv, w_kv, k_norm_b, norm_b, w_q_b, q_norm_b, w_out_b, norm_moe, w_router_group, b_router_group, w_router_expert, b_router_expert, w_gate, w_up, w_down):
    batch, seq, d = x.shape
    depth = norm_moe.shape[0]
    n_a = norm_a.shape[0]
    n_kheads = d // HEAD_DIM
    kdim = n_kheads * HEAD_DIM
    n_main = 6 * kdim
    h = x.reshape(batch * seq, d)
    hn = None
    kv = None
    for layer in range(depth):
        if layer < n_a:
            i = layer
            w_in_t = jnp.swapaxes(w_in_a[i], 0, 1)
            xn, gates = gdn_gates(h, norm_a[i], w_in_t, n_main, a_log_a[i], dt_bias_a[i])
            proj = matmul(xn, w_in_t, n_cols=n_main, w_transposed=True, tm=1024, tn=1024)
            o = gdn_core(proj, gates, conv_a[i], out_norm_a[i], batch=batch, seq=seq, n_kheads=n_kheads)
            h = matmul(o, w_out_a[i], res=h, tm=1024, tn=512)
        else:
            i = layer - n_a
            if i == 0:
                kv = matmul(hn, w_kv, scale=norm_kv, tm=1024, tn=1024)
            q = matmul(hn, w_q_b[i], scale=norm_b[i], tm=1024, tn=1024)
            o = sb_attention(q, kv, q_norm_b[i], k_norm_b, batch=batch, seq=seq, n_heads=n_kheads)
            h = matmul(o, w_out_b[i], res=h, tm=1024, tn=1024)
        emit_norm = n_a <= layer + 1 < depth
        out = hierarchical_moe(h, norm_moe[layer], w_router_group[layer], b_router_group[layer],
                               w_router_expert[layer], b_router_expert[layer],
                               w_gate, w_up, w_down, layer=layer, emit_norm=emit_norm)
        h, hn = (out[0], out[1]) if emit_norm else (out[0], None)
    return h.reshape(batch, seq, d)
```
